```python
import math
import jax, jax.numpy as jnp
from jax import lax
import numpy as np

D_MODEL = 1024
BATCH = 16
SEQ = 2048
DEPTH = 2
DEC_BATCH = 32
DEC_SEQ = 4
PAST_LEN = 16384
PAGE_SIZE = 128

N_A_LAYERS = DEPTH // 2
N_B_LAYERS = DEPTH - N_A_LAYERS
D_RNN = 1536
N_RG_BLOCKS = 12
RG_BLOCK = D_RNN // N_RG_BLOCKS
CONV_W = 4
RG_C = 8.0
N_HEADS = 16
HEAD_DIM = D_MODEL // N_HEADS
D_ATT = N_HEADS * HEAD_DIM
Q_BLOCK = 128
SB_BIAS_HI = -2.0
SB_BIAS_LO = -12.0
DN_ALPHA = (2.0 * DEPTH) ** 0.25
DN_BETA = (8.0 * DEPTH) ** -0.25
LN_EPS = 1e-5
POOL_NUM = 5
POOL_DEN = 4

kernel_name = "yoco_hawk_stickbreak_decoder_step"


def layer_norm(x, g, b):
    xf = x.astype(jnp.float32)
    mu = jnp.mean(xf, axis=-1, keepdims=True)
    var = jnp.mean(jnp.square(xf - mu), axis=-1, keepdims=True)
    y = (xf - mu) * lax.rsqrt(var + LN_EPS) * g.astype(jnp.float32) + b.astype(jnp.float32)
    return y.astype(x.dtype)


def ada_mod(c, w, b, n):
    m = jax.nn.silu(c) @ w + b
    return jnp.split(m[:, None, :], n, axis=-1)


def rglru_mixer(u, h0, conv0, w_in, conv_w, conv_b, w_rg, b_rg, w_ig, b_ig, lam, w_out):
    bn, t, _ = u.shape
    xb, gate = jnp.split(u @ w_in, 2, axis=-1)
    xpad = jnp.concatenate([conv0.astype(xb.dtype), xb], axis=1)
    xc = conv_b + sum(xpad[:, k:k + t] * conv_w[k] for k in range(CONV_W))
    new_conv = xpad[:, t:]
    xblk = xc.reshape(bn, t, N_RG_BLOCKS, RG_BLOCK)
    r = jax.nn.sigmoid(jnp.einsum('btni,nio->btno', xblk, w_rg).reshape(bn, t, D_RNN) + b_rg)
    i = jax.nn.sigmoid(jnp.einsum('btni,nio->btno', xblk, w_ig).reshape(bn, t, D_RNN) + b_ig)
    log_a = (RG_C * r.astype(jnp.float32)) * jax.nn.log_sigmoid(lam.astype(jnp.float32))
    a = jnp.exp(log_a)
    b_term = jnp.sqrt(-jnp.expm1(2.0 * log_a)) * (i * xc).astype(jnp.float32)
    b_term = b_term.at[:, 0].add(a[:, 0] * h0.astype(jnp.float32))

    def combine(left, right):
        a1, b1 = left
        a2, b2 = right
        return a1 * a2, a2 * b1 + b2

    _, h = lax.associative_scan(combine, (a, b_term), axis=1)
    y = (h.astype(u.dtype) * jax.nn.silu(gate)) @ w_out
    return y, h[:, -1], new_conv


def stick_breaking_attention(q, k, v, bias, q_offset):
    t_len = q.shape[1]
    blk = min(Q_BLOCK, t_len)
    bias_f = bias.astype(jnp.float32)[None, :, None, None]
    outs = []
    for start in range(0, t_len, blk):
        stop = min(start + blk, t_len)
        n_keys = max(q_offset + stop - 1, 1)
        z = jnp.einsum('bqhd,bkhd->bhqk', q[:, start:stop], k[:, :n_keys],
                       preferred_element_type=jnp.float32) * (HEAD_DIM ** -0.5) + bias_f
        q_pos = q_offset + jnp.arange(start, stop)
        k_pos = jnp.arange(n_keys)
        causal = k_pos[None, :] < q_pos[:, None]
        log_1mb = jnp.where(causal, jax.nn.log_sigmoid(-z), 0.0)
        suffix = lax.cumsum(log_1mb, axis=3, reverse=True)
        after = jnp.concatenate([suffix[..., 1:], jnp.zeros_like(suffix[..., :1])], axis=-1)
        w = jnp.where(causal, jnp.exp(jax.nn.log_sigmoid(z) + after), 0.0)
        outs.append(jnp.einsum('bhqk,bkhd->bqhd', w, v[:, :n_keys].astype(jnp.float32)))
    return jnp.concatenate(outs, axis=1).astype(q.dtype)


def sb_mixer(u, k, v, q_offset, w_in, bias, w_out):
    bn, t, _ = u.shape
    q, gate = jnp.split(u @ w_in, 2, axis=-1)
    o = stick_breaking_attention(q.reshape(bn, t, N_HEADS, HEAD_DIM), k, v, bias, q_offset)
    return (o.reshape(bn, t, D_ATT) * jax.nn.silu(gate)) @ w_out


def shared_kv(x, c, ada_w_kv, ada_b_kv, w_k, w_v):
    bn, t, _ = x.shape
    shift, scale = ada_mod(c, ada_w_kv, ada_b_kv, 2)
    u = x * (1 + scale) + shift
    k = (u @ w_k).reshape(bn, t, N_HEADS, HEAD_DIM)
    v = (u @ w_v).reshape(bn, t, N_HEADS, HEAD_DIM)
    return k, v


def run_group(x, c, h_state, conv_state, past_k, past_v, p):
    q_offset = 0 if past_k is None else past_k.shape[1]
    new_h, new_conv = [], []
    k_new = v_new = k_all = v_all = None
    for l in range(DEPTH):
        if l < N_A_LAYERS:
            shift, scale, gate = ada_mod(c, p['ada_w_a'][l], p['ada_b_a'][l], 3)
            u = x * (1 + scale) + shift
            y, h_l, conv_l = rglru_mixer(u, h_state[l], conv_state[l], p['w_in_a'][l], p['conv_w'][l],
                                         p['conv_b'][l], p['w_rg'][l], p['b_rg'][l], p['w_ig'][l],
                                         p['b_ig'][l], p['lru_lambda'][l], p['w_out_a'][l])
            new_h.append(h_l)
            new_conv.append(conv_l)
            x = layer_norm(DN_ALPHA * x + (1 + gate) * y, p['ln_g_a'][l], p['ln_b_a'][l])
        else:
            if l == N_A_LAYERS:
                k_new, v_new = shared_kv(x, c, p['ada_w_kv'], p['ada_b_kv'], p['w_k'], p['w_v'])
                if past_k is None:
                    k_all, v_all = k_new, v_new
                else:
                    k_all = jnp.concatenate([past_k.astype(k_new.dtype), k_new], axis=1)
                    v_all = jnp.concatenate([past_v.astype(v_new.dtype), v_new], axis=1)
            j = l - N_A_LAYERS
            shift, scale, gate = ada_mod(c, p['ada_w_b'][j], p['ada_b_b'][j], 3)
            u = x * (1 + scale) + shift
            y = sb_mixer(u, k_all, v_all, q_offset, p['w_in_b'][j], p['sb_bias'][j], p['w_out_b'][j])
            x = layer_norm(DN_ALPHA * x + (1 + gate) * y, p['ln_g_b'][j], p['ln_b_b'][j])
    return x, jnp.stack(new_h), jnp.stack(new_conv), k_new, v_new


def setup_inputs(seed: int = 0) -> dict:
    key = jax.random.key(seed)
    ks = iter(jax.random.split(key, 48))
    f32 = jnp.float32

    def nrm(shape, scale=1.0):
        return jax.random.normal(next(ks), shape, f32) * scale

    n_pages = PAST_LEN // PAGE_SIZE
    n_used = DEC_BATCH * n_pages
    n_pool = (n_used * POOL_NUM) // POOL_DEN
    page_table = jax.random.permutation(next(ks), n_pool)[:n_used].reshape(DEC_BATCH, n_pages).astype(jnp.int32)
    a_c = jax.random.uniform(next(ks), (N_A_LAYERS, D_RNN), f32, 0.9, 0.999)
    root = a_c ** (1.0 / RG_C)
    lru_lambda = jnp.log(root) - jnp.log1p(-root)
    ada_s = 0.2 * D_MODEL ** -0.5
    sb_bias = jnp.broadcast_to(jnp.linspace(SB_BIAS_HI, SB_BIAS_LO, N_HEADS, dtype=f32),
                               (N_B_LAYERS, N_HEADS)) + nrm((N_B_LAYERS, N_HEADS), 0.1)
    return {
        "x_prompt": nrm((BATCH, SEQ, D_MODEL)),
        "x_sample": nrm((DEC_BATCH, DEC_SEQ, D_MODEL)),
        "c_prompt": nrm((BATCH, D_MODEL)),
        "c_sample": nrm((DEC_BATCH, D_MODEL)),
        "state_lru_h": nrm((N_A_LAYERS, DEC_BATCH, D_RNN), 0.5),
        "state_lru_conv": nrm((N_A_LAYERS, DEC_BATCH, CONV_W - 1, D_RNN)),
        "cache_k": nrm((n_pool, PAGE_SIZE, N_HEADS, HEAD_DIM)),
        "cache_v": nrm((n_pool, PAGE_SIZE, N_HEADS, HEAD_DIM), DN_BETA),
        "page_table": page_table,
        "ada_w_a": nrm((N_A_LAYERS, D_MODEL, 3 * D_MODEL), ada_s),
        "ada_b_a": nrm((N_A_LAYERS, 3 * D_MODEL), 0.02),
        "w_in_a": nrm((N_A_LAYERS, D_MODEL, 2 * D_RNN), D_MODEL ** -0.5),
        "conv_w": nrm((N_A_LAYERS, CONV_W, D_RNN), CONV_W ** -0.5),
        "conv_b": nrm((N_A_LAYERS, D_RNN), 0.02),
        "w_rg": nrm((N_A_LAYERS, N_RG_BLOCKS, RG_BLOCK, RG_BLOCK), RG_BLOCK ** -0.5),
        "b_rg": nrm((N_A_LAYERS, D_RNN), 0.02),
        "w_ig": nrm((N_A_LAYERS, N_RG_BLOCKS, RG_BLOCK, RG_BLOCK), RG_BLOCK ** -0.5),
        "b_ig": nrm((N_A_LAYERS, D_RNN), 0.02),
        "lru_lambda": lru_lambda,
        "w_out_a": nrm((N_A_LAYERS, D_RNN, D_MODEL), DN_BETA * D_RNN ** -0.5),
        "ln_g_a": 1.0 + nrm((N_A_LAYERS, D_MODEL), 0.02),
        "ln_b_a": nrm((N_A_LAYERS, D_MODEL), 0.02),
        "ada_w_kv": nrm((D_MODEL, 2 * D_MODEL), ada_s),
        "ada_b_kv": nrm((2 * D_MODEL,), 0.02),
        "w_k": nrm((D_MODEL, D_ATT), D_MODEL ** -0.5),
        "w_v": nrm((D_MODEL, D_ATT), DN_BETA * D_MODEL ** -0.5),
        "ada_w_b": nrm((N_B_LAYERS, D_MODEL, 3 * D_MODEL), ada_s),
        "ada_b_b": nrm((N_B_LAYERS, 3 * D_MODEL), 0.02),
        "w_in_b": nrm((N_B_LAYERS, D_MODEL, 2 * D_ATT), D_MODEL ** -0.5),
        "sb_bias": sb_bias,
        "w_out_b": nrm((N_B_LAYERS, D_ATT, D_MODEL), DN_BETA * D_ATT ** -0.5),
        "ln_g_b": 1.0 + nrm((N_B_LAYERS, D_MODEL), 0.02),
        "ln_b_b": nrm((N_B_LAYERS, D_MODEL), 0.02),
    }


def reference(x_prompt, x_sample, c_prompt, c_sample, state_lru_h, state_lru_conv, cache_k, cache_v,
              page_table, ada_w_a, ada_b_a, w_in_a, conv_w, conv_b, w_rg, b_rg, w_ig, b_ig, lru_lambda,
              w_out_a, ln_g_a, ln_b_a, ada_w_kv, ada_b_kv, w_k, w_v, ada_w_b, ada_b_b, w_in_b, sb_bias,
              w_out_b, ln_g_b, ln_b_b):
    params = dict(ada_w_a=ada_w_a, ada_b_a=ada_b_a, w_in_a=w_in_a, conv_w=conv_w, conv_b=conv_b,
                  w_rg=w_rg, b_rg=b_rg, w_ig=w_ig, b_ig=b_ig, lru_lambda=lru_lambda, w_out_a=w_out_a,
                  ln_g_a=ln_g_a, ln_b_a=ln_b_a, ada_w_kv=ada_w_kv, ada_b_kv=ada_b_kv, w_k=w_k, w_v=w_v,
                  ada_w_b=ada_w_b, ada_b_b=ada_b_b, w_in_b=w_in_b, sb_bias=sb_bias, w_out_b=w_out_b,
                  ln_g_b=ln_g_b, ln_b_b=ln_b_b)
    bp = x_prompt.shape[0]
    h0 = jnp.zeros((N_A_LAYERS, bp, D_RNN), x_prompt.dtype)
    conv0 = jnp.zeros((N_A_LAYERS, bp, CONV_W - 1, D_RNN), x_prompt.dtype)
    y_prompt, h_prompt, conv_prompt, k_prompt, v_prompt = run_group(
        x_prompt, c_prompt, h0, conv0, None, None, params)
    bs = x_sample.shape[0]
    past_k = cache_k[page_table].reshape(bs, -1, N_HEADS, HEAD_DIM)
    past_v = cache_v[page_table].reshape(bs, -1, N_HEADS, HEAD_DIM)
    y_sample, h_sample, conv_sample, k_sample, v_sample = run_group(
        x_sample, c_sample, state_lru_h, state_lru_conv, past_k, past_v, params)
    return (y_prompt, y_sample, h_prompt, conv_prompt, k_prompt, v_prompt,
            h_sample, conv_sample, k_sample, v_sample)
```

```python
import functools
import math

import jax
import jax.numpy as jnp
from jax import lax
from jax.experimental import pallas as pl
from jax.experimental.pallas import tpu as pltpu

HEAD_DIM = 64
RG_BLOCK = 128
RG_C = 8.0
CONV_W = 4
LN_EPS = 1e-5
LANES = 128
SUBLANES = 8
PAGES_PER_STEP = 8
VMEM_LIMIT = 56 * 1024 * 1024

BF16 = jnp.bfloat16
F32 = jnp.float32


def _dot(a, b):
    return jnp.dot(a, b, preferred_element_type=F32)


def _dot_nt(a, b):
    return lax.dot_general(a, b, (((1,), (1,)), ((), ())), preferred_element_type=F32)


def _sigmoid(x):
    return 1.0 / (1.0 + jnp.exp(-x))


def _softplus(x):
    return jnp.maximum(x, 0.0) + jnp.log1p(jnp.exp(-jnp.abs(x)))


def _layer_norm(z, g, b):
    mu = jnp.mean(z, axis=-1, keepdims=True)
    zc = z - mu
    var = jnp.mean(zc * zc, axis=-1, keepdims=True)
    return zc * lax.rsqrt(var + LN_EPS) * g + b


def _split_hi_lo(x):
    hi = x.astype(BF16)
    lo = (x - hi.astype(F32)).astype(BF16)
    return hi, lo


def _params(*sem):
    return pltpu.CompilerParams(dimension_semantics=sem, vmem_limit_bytes=VMEM_LIMIT)


def _const_spec(shape):
    zeros = (0,) * len(shape)
    return pl.BlockSpec(shape, lambda *_: zeros)


def _ada_kernel(c_ref, w_ref, b_ref, o_ref):
    c = c_ref[...]
    s = (c * _sigmoid(c)).astype(BF16)
    o_ref[...] = _dot(s, w_ref[...].astype(BF16)) + b_ref[...]


def _ada(c, w, b):
    n, d = c.shape
    width = w.shape[1]
    tn = 1024 if width % 1024 == 0 else width
    return pl.pallas_call(
        _ada_kernel,
        grid=(width // tn,),
        in_specs=[
            pl.BlockSpec((n, d), lambda j: (0, 0)),
            pl.BlockSpec((d, tn), lambda j: (0, j)),
            pl.BlockSpec((1, tn), lambda j: (0, j)),
        ],
        out_specs=pl.BlockSpec((n, tn), lambda j: (0, j)),
        out_shape=jax.ShapeDtypeStruct((n, width), F32),
        compiler_params=_params("arbitrary"),
        name="ada_mod",
    )(c, w, b.reshape(1, width))


def _rglru_kernel(x_ref, mod_ref, h0_ref, conv0_ref, w_in_ref, convw_ref, convb_ref, wg_ref,
                  brg_ref, big_ref, lam_ref, w_out_ref, lng_ref, lnb_ref,
                  x1_ref, hlast_ref, convlast_ref,
                  xpad_ref, sg_ref, a_ref, b_ref, hs_ref, h_ref, *, tc, bg, d, dr, alpha):
    rows = tc * bg
    halo = (CONV_W - 1) * bg

    @pl.when(pl.program_id(0) == 0)
    def _():
        h_ref[...] = h0_ref[...]
        xpad_ref[0:halo, :] = conv0_ref[...].reshape(halo, dr)

    shift = mod_ref[:, 0:d]
    scale = mod_ref[:, d:2 * d]
    gate_c = mod_ref[:, 2 * d:3 * d]
    x = x_ref[...]
    u = x * (1.0 + scale)[None] + shift[None]
    xg = _dot(u.reshape(rows, d).astype(BF16), w_in_ref[...])
    gate = xg[:, dr:]
    sg_ref[...] = gate * _sigmoid(gate)
    xpad_ref[halo:halo + rows, :] = xg[:, :dr]

    lam = lam_ref[...]
    log_sig_lam = -_softplus(-lam)
    for n in range(dr // RG_BLOCK):
        sl = slice(n * RG_BLOCK, (n + 1) * RG_BLOCK)
        xc = convb_ref[:, sl] + sum(
            xpad_ref[k * bg:k * bg + rows, sl] * convw_ref[k:k + 1, sl] for k in range(CONV_W))
        pre = _dot(xc.astype(BF16), wg_ref[n])
        r = _sigmoid(pre[:, :RG_BLOCK] + brg_ref[:, sl])
        ig = _sigmoid(pre[:, RG_BLOCK:] + big_ref[:, sl])
        log_a = (RG_C * r) * log_sig_lam[:, sl]
        a = jnp.exp(log_a)
        a_ref[:, sl] = a
        b_ref[:, sl] = jnp.sqrt(1.0 - a * a) * (ig * xc)

    h = h_ref[...]
    for t in range(tc):
        h = a_ref[t * bg:(t + 1) * bg, :] * h + b_ref[t * bg:(t + 1) * bg, :]
        hs_ref[t * bg:(t + 1) * bg, :] = h
    h_ref[...] = h
    hlast_ref[...] = h
    tail = xpad_ref[rows:rows + halo, :]
    convlast_ref[...] = tail.reshape(CONV_W - 1, bg, dr)
    xpad_ref[0:halo, :] = tail

    y = _dot((hs_ref[...] * sg_ref[...]).astype(BF16), w_out_ref[...])
    z = alpha * x + (1.0 + gate_c)[None] * y.reshape(tc, bg, d)
    x1_ref[...] = _layer_norm(z, lng_ref[...][None], lnb_ref[...][None])


def _rglru_layer(xt, mods, h0, conv0t, w_in, conv_w, conv_b, wg, b_rg, b_ig, lam, w_out, ln_g,
                 ln_b, *, alpha, tc):
    t_len, bg, d = xt.shape
    dr = w_out.shape[0]
    rows = tc * bg
    halo = (CONV_W - 1) * bg
    nblk = dr // RG_BLOCK
    kern = functools.partial(_rglru_kernel, tc=tc, bg=bg, d=d, dr=dr, alpha=alpha)
    return pl.pallas_call(
        kern,
        grid=(t_len // tc,),
        in_specs=[
            pl.BlockSpec((tc, bg, d), lambda i: (i, 0, 0)),
            _const_spec((bg, 3 * d)),
            _const_spec((bg, dr)),
            _const_spec((CONV_W - 1, bg, dr)),
            _const_spec((d, 2 * dr)),
            _const_spec((CONV_W, dr)),
            _const_spec((1, dr)),
            _const_spec((nblk, RG_BLOCK, 2 * RG_BLOCK)),
            _const_spec((1, dr)),
            _const_spec((1, dr)),
            _const_spec((1, dr)),
            _const_spec((dr, d)),
            _const_spec((1, d)),
            _const_spec((1, d)),
        ],
        out_specs=[
            pl.BlockSpec((tc, bg, d), lambda i: (i, 0, 0)),
            _const_spec((bg, dr)),
            _const_spec((CONV_W - 1, bg, dr)),
        ],
        out_shape=[
            jax.ShapeDtypeStruct((t_len, bg, d), F32),
            jax.ShapeDtypeStruct((bg, dr), F32),
            jax.ShapeDtypeStruct((CONV_W - 1, bg, dr), F32),
        ],
        scratch_shapes=[
            pltpu.VMEM((rows + halo, dr), F32),
            pltpu.VMEM((rows, dr), F32),
            pltpu.VMEM((rows, dr), F32),
            pltpu.VMEM((rows, dr), F32),
            pltpu.VMEM((rows, dr), F32),
            pltpu.VMEM((bg, dr), F32),
        ],
        compiler_params=_params("arbitrary"),
        name="rglru_layer",
    )(xt, mods, h0, conv0t, w_in, conv_w, conv_b.reshape(1, dr), wg, b_rg.reshape(1, dr),
      b_ig.reshape(1, dr), lam.reshape(1, dr), w_out, ln_g.reshape(1, d), ln_b.reshape(1, d))


def _proj_kernel(x_ref, mkv_ref, mb_ref, wkv_ref, wqg_ref,
                 k_ref, v_ref, kb_ref, vb_ref, q_ref, g_ref, *, d, da):
    x = x_ref[...]
    u = (x * (1.0 + mkv_ref[:, d:2 * d]) + mkv_ref[:, 0:d]).astype(BF16)
    kv = _dot(u, wkv_ref[...])
    k = kv[:, :da]
    v = kv[:, da:]
    k_ref[...] = k
    v_ref[...] = v
    kb_ref[...] = k.astype(BF16)
    vb_ref[...] = v.astype(BF16)
    u2 = (x * (1.0 + mb_ref[:, d:2 * d]) + mb_ref[:, 0:d]).astype(BF16)
    qg = _dot(u2, wqg_ref[...])
    q_ref[...] = (qg[:, :da] * (HEAD_DIM ** -0.5)).astype(BF16)
    gate = qg[:, da:]
    g_ref[...] = gate * _sigmoid(gate)


def _proj(x, mkv, mb, wkv, wqg, *, tm):
    n, d = x.shape
    da = wkv.shape[1] // 2
    groups, mrows, _ = mkv.shape
    per_group = n // groups // tm
    kern = functools.partial(_proj_kernel, d=d, da=da)
    row_spec = pl.BlockSpec((tm, da), lambda g, i: (g * per_group + i, 0))
    return pl.pallas_call(
        kern,
        grid=(groups, per_group),
        in_specs=[
            pl.BlockSpec((tm, d), lambda g, i: (g * per_group + i, 0)),
            pl.BlockSpec((None, mrows, mkv.shape[2]), lambda g, i: (g, 0, 0)),
            pl.BlockSpec((None, mrows, mb.shape[2]), lambda g, i: (g, 0, 0)),
            _const_spec(wkv.shape),
            _const_spec(wqg.shape),
        ],
        out_specs=[row_spec] * 6,
        out_shape=[
            jax.ShapeDtypeStruct((n, da), F32),
            jax.ShapeDtypeStruct((n, da), F32),
            jax.ShapeDtypeStruct((n, da), BF16),
            jax.ShapeDtypeStruct((n, da), BF16),
            jax.ShapeDtypeStruct((n, da), BF16),
            jax.ShapeDtypeStruct((n, da), F32),
        ],
        compiler_params=_params("parallel", "arbitrary"),
        name="kv_q_proj",
    )(x, mkv, mb, wkv, wqg)


def _attn_kernel(bias_ref, q_ref, k_ref, v_ref, tri_ref, o_ref, c_ref, *, tq):
    pair = pl.program_id(1)
    i = pl.program_id(2)
    lane = lax.broadcasted_iota(jnp.int32, (1, LANES), 1)
    q = q_ref[...]
    tri = tri_ref[...]
    rel = (lax.broadcasted_iota(jnp.int32, (tq, tq), 1)
           < lax.broadcasted_iota(jnp.int32, (tq, tq), 0))

    o_ref[...] = jnp.zeros_like(o_ref)
    for hh in range(2):
        head_lanes = (lane < HEAD_DIM) if hh == 0 else (lane >= HEAD_DIM)
        qh = jnp.where(head_lanes, q, jnp.zeros_like(q))
        bias = bias_ref[0, 2 * pair + hh]

        def block(j, causal):
            koff = pl.multiple_of(j * tq, tq)
            kblk = k_ref[pl.ds(koff, tq), :]
            vblk = v_ref[pl.ds(koff, tq), :]
            vblk = jnp.where(head_lanes, vblk, jnp.zeros_like(vblk))
            z = _dot_nt(qh, kblk) + bias
            sp = _softplus(z)
            log_1mb = -sp
            if causal is not None:
                log_1mb = jnp.where(causal, log_1mb, 0.0)
            hi, lo = _split_hi_lo(log_1mb)
            after = _dot(hi, tri) + _dot(lo, tri) + c_ref[...]
            w = jnp.exp(z - sp + after)
            if causal is not None:
                w = jnp.where(causal, w, 0.0)
            o_ref[...] += _dot(w.astype(BF16), vblk)
            c_ref[...] += jnp.sum(log_1mb, axis=1, keepdims=True)

        c_ref[...] = jnp.zeros_like(c_ref)
        block(i, rel)

        def body(s, carry):
            block(i - 1 - s, None)
            return carry

        lax.fori_loop(0, i, body, 0)


def _attention(q, kb, vb, bias, *, batch, tq):
    n, da = q.shape
    t_len = n // batch
    nq = t_len // tq
    pairs = da // LANES
    tri = (lax.broadcasted_iota(jnp.int32, (tq, tq), 0)
           > lax.broadcasted_iota(jnp.int32, (tq, tq), 1)).astype(BF16)
    kern = functools.partial(_attn_kernel, tq=tq)
    return pl.pallas_call(
        kern,
        grid=(batch, pairs, nq),
        in_specs=[
            pl.BlockSpec(memory_space=pltpu.SMEM),
            pl.BlockSpec((tq, LANES), lambda b, p, i: (b * nq + i, p)),
            pl.BlockSpec((t_len, LANES), lambda b, p, i: (b, p)),
            pl.BlockSpec((t_len, LANES), lambda b, p, i: (b, p)),
            _const_spec((tq, tq)),
        ],
        out_specs=pl.BlockSpec((tq, LANES), lambda b, p, i: (b * nq + i, p)),
        out_shape=jax.ShapeDtypeStruct((n, da), F32),
        scratch_shapes=[pltpu.VMEM((tq, 1), F32)],
        compiler_params=_params("parallel", "parallel", "arbitrary"),
        name="sb_attention",
    )(bias, q, kb, vb, tri)


def _sattn_kernel(pt_ref, q_ref, kn_ref, vn_ref, bias_ref, tri_ref, *refs, n_new, page):
    del pt_ref
    k_refs = refs[:PAGES_PER_STEP]
    v_refs = refs[PAGES_PER_STEP:2 * PAGES_PER_STEP]
    o_ref = refs[2 * PAGES_PER_STEP]
    acc_ref, c_ref, qbd_ref, kpad_ref, vpad_ref = refs[2 * PAGES_PER_STEP + 1:]
    g = pl.program_id(1)
    ncol, da = qbd_ref.shape
    heads = da // HEAD_DIM
    tri = tri_ref[...]

    def visit(kf, vf, valid):
        z = _dot_nt(kf.astype(BF16), qbd_ref[...]) + bias_ref[...]
        sp = _softplus(z)
        log_1mb = -sp
        if valid is not None:
            log_1mb = jnp.where(valid, log_1mb, 0.0)
        hi, lo = _split_hi_lo(log_1mb)
        after = _dot(tri, hi) + _dot(tri, lo) + c_ref[...]
        w = jnp.exp(z - sp + after)
        if valid is not None:
            w = jnp.where(valid, w, 0.0)
        acc_ref[...] += _dot(w.T.astype(BF16), vf.astype(BF16))
        c_ref[...] += jnp.sum(log_1mb, axis=0, keepdims=True)

    @pl.when(g == 0)
    def _():
        q = q_ref[...]
        qrep = jnp.concatenate(
            [jnp.broadcast_to(q[t:t + 1, :], (heads, da)) for t in range(n_new)], axis=0)
        own = (lax.broadcasted_iota(jnp.int32, (ncol, da), 0) % heads
               == lax.broadcasted_iota(jnp.int32, (ncol, da), 1) // HEAD_DIM)
        qbd_ref[...] = jnp.where(own, qrep, 0.0).astype(BF16)
        acc_ref[...] = jnp.zeros_like(acc_ref)
        c_ref[...] = jnp.zeros_like(c_ref)
        kpad_ref[...] = jnp.zeros_like(kpad_ref)
        vpad_ref[...] = jnp.zeros_like(vpad_ref)
        kpad_ref[0:SUBLANES, :] = kn_ref[...]
        vpad_ref[0:SUBLANES, :] = vn_ref[...]
        valid = (lax.broadcasted_iota(jnp.int32, (page, ncol), 0)
                 < lax.broadcasted_iota(jnp.int32, (page, ncol), 1) // heads)
        visit(kpad_ref[...], vpad_ref[...], valid)

    for r in range(PAGES_PER_STEP):
        visit(k_refs[r][...], v_refs[r][...], None)

    @pl.when(g == pl.num_programs(1) - 1)
    def _():
        own = (lax.broadcasted_iota(jnp.int32, (ncol, da), 0) % heads
               == lax.broadcasted_iota(jnp.int32, (ncol, da), 1) // HEAD_DIM)
        acc = jnp.where(own, acc_ref[...], 0.0)
        o_ref[...] = jnp.zeros_like(o_ref)
        for t in range(n_new):
            o_ref[t:t + 1, :] = jnp.sum(acc[t * heads:(t + 1) * heads, :], axis=0, keepdims=True)


def _sample_attention(q8, kn8, vn8, bias_cols, cache_k, cache_v, page_table, *, n_new):
    bs, _, da = q8.shape
    n_pool, page, _ = cache_k.shape
    n_pages = page_table.shape[1]
    heads = da // HEAD_DIM
    ncol = n_new * heads
    steps = n_pages // PAGES_PER_STEP
    tri = (lax.broadcasted_iota(jnp.int32, (page, page), 1)
           > lax.broadcasted_iota(jnp.int32, (page, page), 0)).astype(BF16)

    def page_spec(r):
        def index(b, g, pt):
            return (pt[b * n_pages + (n_pages - 1 - (g * PAGES_PER_STEP + r))], 0, 0)
        return pl.BlockSpec((None, page, da), index)

    tok_spec = pl.BlockSpec((None, SUBLANES, da), lambda b, g, pt: (b, 0, 0))
    kern = functools.partial(_sattn_kernel, n_new=n_new, page=page)
    grid_spec = pltpu.PrefetchScalarGridSpec(
        num_scalar_prefetch=1,
        grid=(bs, steps),
        in_specs=[tok_spec, tok_spec, tok_spec,
                  pl.BlockSpec((1, ncol), lambda b, g, pt: (0, 0)),
                  pl.BlockSpec((page, page), lambda b, g, pt: (0, 0))]
                 + [page_spec(r) for r in range(PAGES_PER_STEP)] * 2,
        out_specs=tok_spec,
        scratch_shapes=[
            pltpu.VMEM((ncol, da), F32),
            pltpu.VMEM((1, ncol), F32),
            pltpu.VMEM((ncol, da), BF16),
            pltpu.VMEM((page, da), F32),
            pltpu.VMEM((page, da), F32),
        ],
    )
    return pl.pallas_call(
        kern,
        grid_spec=grid_spec,
        out_shape=jax.ShapeDtypeStruct((bs, SUBLANES, da), F32),
        compiler_params=_params("parallel", "arbitrary"),
        name="sb_attention_paged",
    )(page_table.reshape(-1), q8, kn8, vn8, bias_cols, tri,
      *([cache_k] * PAGES_PER_STEP), *([cache_v] * PAGES_PER_STEP))


def _out_kernel(o_ref, g_ref, x_ref, gc_ref, w_ref, lng_ref, lnb_ref, y_ref, *, alpha):
    y = _dot((o_ref[...] * g_ref[...]).astype(BF16), w_ref[...])
    z = alpha * x_ref[...] + (1.0 + gc_ref[...]) * y
    y_ref[...] = _layer_norm(z, lng_ref[...], lnb_ref[...])


def _out_layer(o, g, x, gate_c, w, ln_g, ln_b, *, alpha, tm):
    n, d = x.shape
    da = o.shape[1]
    groups, mrows, _ = gate_c.shape
    per_group = n // groups // tm
    kern = functools.partial(_out_kernel, alpha=alpha)

    def rows(width):
        return pl.BlockSpec((tm, width), lambda gi, i: (gi * per_group + i, 0))

    return pl.pallas_call(
        kern,
        grid=(groups, per_group),
        in_specs=[rows(da), rows(da), rows(d),
                  pl.BlockSpec((None, mrows, d), lambda gi, i: (gi, 0, 0)),
                  _const_spec(w.shape), _const_spec((1, d)), _const_spec((1, d))],
        out_specs=rows(d),
        out_shape=jax.ShapeDtypeStruct((n, d), F32),
        compiler_params=_params("parallel", "arbitrary"),
        name="attn_out_ln",
    )(o, g, x, gate_c, w, ln_g.reshape(1, d), ln_b.reshape(1, d))


def _largest_divisor(n, cap):
    return max(k for k in range(1, cap + 1) if n % k == 0)


def kernel(x_prompt, x_sample, c_prompt, c_sample, state_lru_h, state_lru_conv, cache_k, cache_v, page_table, ada_w_a, ada_b_a, w_in_a, conv_w, conv_b, w_rg, b_rg, w_ig, b_ig, lru_lambda, w_out_a, ln_g_a, ln_b_a, ada_w_kv, ada_b_kv, w_k, w_v, ada_w_b, ada_b_b, w_in_b, sb_bias, w_out_b, ln_g_b, ln_b_b):
    assert ada_w_a.shape[0] == 1 and ada_w_b.shape[0] == 1, "one RG-LRU and one attention layer"
    bp, t_len, d = x_prompt.shape
    bs, n_new, _ = x_sample.shape
    dr = w_out_a.shape[1]
    da = w_k.shape[1]
    heads = sb_bias.shape[1]
    assert da == heads * HEAD_DIM and w_rg.shape[2] == RG_BLOCK and conv_w.shape[1] == CONV_W
    assert bp % SUBLANES == 0 and bs % SUBLANES == 0 and n_new <= SUBLANES
    n_pool, page = cache_k.shape[:2]
    alpha = (2.0 * (ada_w_a.shape[0] + ada_w_b.shape[0])) ** 0.25

    c_all = jnp.concatenate([c_prompt, c_sample], axis=0)
    mod_a = _ada(c_all, ada_w_a[0], ada_b_a[0])
    mod_kv = _ada(c_all, ada_w_kv, ada_b_kv)
    mod_b = _ada(c_all, ada_w_b[0], ada_b_b[0])

    w_in = w_in_a[0].astype(BF16)
    wg = jnp.concatenate([w_rg[0], w_ig[0]], axis=-1).astype(BF16)
    w_out = w_out_a[0].astype(BF16)
    wkv = jnp.concatenate([w_k, w_v], axis=1).astype(BF16)
    wqg = w_in_b[0].astype(BF16)
    w_ob = w_out_b[0].astype(BF16)
    rg_args = (w_in, conv_w[0], conv_b[0], wg, b_rg[0], b_ig[0], lru_lambda[0], w_out,
               ln_g_a[0], ln_b_a[0])

    tc = _largest_divisor(t_len, max(1, 512 // bp))
    x1t, h_p, conv_p = _rglru_layer(
        x_prompt.transpose(1, 0, 2), mod_a[:bp], jnp.zeros((bp, dr), F32),
        jnp.zeros((CONV_W - 1, bp, dr), F32), *rg_args, alpha=alpha, tc=tc)
    x1 = x1t.transpose(1, 0, 2).reshape(bp * t_len, d)
    tm = _largest_divisor(t_len, 512)
    k_p, v_p, kb, vb, q, g = _proj(
        x1, mod_kv[:bp, None, :], mod_b[:bp, None, :2 * d], wkv, wqg, tm=tm)
    tq = _largest_divisor(t_len, 256)
    o = _attention(q, kb, vb, sb_bias, batch=bp, tq=tq)
    y_p = _out_layer(o, g, x1, mod_b[:bp, None, 2 * d:], w_ob, ln_g_b[0], ln_b_b[0],
                     alpha=alpha, tm=tm)

    ns = n_new * bs
    x1s_t, h_s, conv_s = _rglru_layer(
        x_sample.transpose(1, 0, 2), mod_a[bp:], state_lru_h[0],
        state_lru_conv[0].transpose(1, 0, 2), *rg_args, alpha=alpha, tc=n_new)
    x1s = x1s_t.reshape(ns, d)
    mkv_s = jnp.tile(mod_kv[bp:], (n_new, 1))[None]
    mb_s = jnp.tile(mod_b[bp:], (n_new, 1))[None]
    k_s, v_s, _, _, q_s, g_s = _proj(x1s, mkv_s, mb_s[:, :, :2 * d], wkv, wqg, tm=ns)

    def per_sequence(a):
        a = a.astype(F32).reshape(n_new, bs, da).transpose(1, 0, 2)
        return jnp.pad(a, ((0, 0), (0, SUBLANES - n_new), (0, 0)))

    bias_cols = jnp.tile(sb_bias[0], n_new).reshape(1, n_new * heads)
    o_s8 = _sample_attention(
        per_sequence(q_s), per_sequence(k_s), per_sequence(v_s), bias_cols,
        cache_k.reshape(n_pool, page, da), cache_v.reshape(n_pool, page, da), page_table,
        n_new=n_new)
    o_s = o_s8[:, :n_new].transpose(1, 0, 2).reshape(ns, da)
    y_s = _out_layer(o_s, g_s, x1s, mb_s[:, :, 2 * d:], w_ob, ln_g_b[0], ln_b_b[0],
                     alpha=alpha, tm=ns)

    def seq_major(a, *tail):
        return a.reshape(n_new, bs, *tail).transpose(1, 0, *range(2, 2 + len(tail)))

    return (
        y_p.reshape(bp, t_len, d),
        seq_major(y_s, d),
        h_p[None],
        conv_p.transpose(1, 0, 2)[None],
        k_p.reshape(bp, t_len, heads, HEAD_DIM),
        v_p.reshape(bp, t_len, heads, HEAD_DIM),
        h_s[None],
        conv_s.transpose(1, 0, 2)[None],
        seq_major(k_s, heads, HEAD_DIM),
        seq_major(v_s, heads, HEAD_DIM),
    )
```

```python
import functools
import math

import jax
import jax.numpy as jnp
from jax import lax
from jax.experimental import pallas as pl
from jax.experimental.pallas import tpu as pltpu

HEAD_DIM = 64
RG_BLOCK = 128
RG_C = 8.0
CONV_W = 4
LN_EPS = 1e-5
LOG2E = math.log2(math.e)
LANES = 128
SUBLANES = 8
PAGES_PER_STEP = 8
VMEM_LIMIT = 56 * 1024 * 1024

BF16 = jnp.bfloat16
F32 = jnp.float32


def _dot(a, b):
    return jnp.dot(a, b, preferred_element_type=F32)


def _dot_nt(a, b):
    return lax.dot_general(a, b, (((1,), (1,)), ((), ())), preferred_element_type=F32)


def _sigmoid(x):
    return 1.0 / (1.0 + jnp.exp(-x))


def _softplus(x):
    return jnp.maximum(x, 0.0) + jnp.log(1.0 + jnp.exp(-jnp.abs(x)))


def _layer_norm(z, g, b):
    mu = jnp.mean(z, axis=-1, keepdims=True)
    zc = z - mu
    var = jnp.mean(zc * zc, axis=-1, keepdims=True)
    return zc * lax.rsqrt(var + LN_EPS) * g + b


def _split_hi_lo(x):
    hi = x.astype(BF16)
    lo = (x - hi.astype(F32)).astype(BF16)
    return hi, lo


def _params(*sem):
    return pltpu.CompilerParams(dimension_semantics=sem, vmem_limit_bytes=VMEM_LIMIT)


def _const_spec(shape):
    zeros = (0,) * len(shape)
    return pl.BlockSpec(shape, lambda *_: zeros)


def _ada_kernel(c_ref, w_ref, b_ref, o_ref):
    c = c_ref[...]
    s = (c * _sigmoid(c)).astype(BF16)
    o_ref[...] = _dot(s, w_ref[...].astype(BF16)) + b_ref[...]


def _ada(c, w, b):
    n, d = c.shape
    width = w.shape[1]
    tn = 1024 if width % 1024 == 0 else width
    return pl.pallas_call(
        _ada_kernel,
        grid=(width // tn,),
        in_specs=[
            pl.BlockSpec((n, d), lambda j: (0, 0)),
            pl.BlockSpec((d, tn), lambda j: (0, j)),
            pl.BlockSpec((1, tn), lambda j: (0, j)),
        ],
        out_specs=pl.BlockSpec((n, tn), lambda j: (0, j)),
        out_shape=jax.ShapeDtypeStruct((n, width), F32),
        compiler_params=_params("arbitrary"),
        name="ada_mod",
    )(c, w, b.reshape(1, width))


def _rglru_kernel(x_ref, mod_ref, h0_ref, conv0_ref, w_in_ref, convw_ref, convb_ref, wg_ref,
                  brg_ref, big_ref, lam_ref, w_out_ref, lng_ref, lnb_ref,
                  x1_ref, hlast_ref, convlast_ref,
                  xpad_ref, sg_ref, a_ref, b_ref, hs_ref, h_ref, *, tc, bg, d, dr, alpha):
    rows = tc * bg
    halo = (CONV_W - 1) * bg

    @pl.when(pl.program_id(0) == 0)
    def _():
        h_ref[...] = h0_ref[...]
        xpad_ref[0:halo, :] = conv0_ref[...].reshape(halo, dr)

    shift = mod_ref[:, 0:d]
    scale = mod_ref[:, d:2 * d]
    gate_c = mod_ref[:, 2 * d:3 * d]
    x = x_ref[...]
    u = x * (1.0 + scale)[None] + shift[None]
    xg = _dot(u.reshape(rows, d).astype(BF16), w_in_ref[...])
    gate = xg[:, dr:]
    sg_ref[...] = gate * _sigmoid(gate)
    xpad_ref[halo:halo + rows, :] = xg[:, :dr]

    lam = lam_ref[...]
    log_sig_lam = -_softplus(-lam)
    for n in range(dr // RG_BLOCK):
        sl = slice(n * RG_BLOCK, (n + 1) * RG_BLOCK)
        xc = convb_ref[:, sl] + sum(
            xpad_ref[k * bg:k * bg + rows, sl] * convw_ref[k:k + 1, sl] for k in range(CONV_W))
        pre = _dot(xc.astype(BF16), wg_ref[n])
        r = _sigmoid(pre[:, :RG_BLOCK] + brg_ref[:, sl])
        ig = _sigmoid(pre[:, RG_BLOCK:] + big_ref[:, sl])
        log_a = (RG_C * r) * log_sig_lam[:, sl]
        a = jnp.exp(log_a)
        a_ref[:, sl] = a
        b_ref[:, sl] = jnp.sqrt(1.0 - a * a) * (ig * xc)

    h = h_ref[...]
    for t in range(tc):
        h = a_ref[t * bg:(t + 1) * bg, :] * h + b_ref[t * bg:(t + 1) * bg, :]
        hs_ref[t * bg:(t + 1) * bg, :] = h
    h_ref[...] = h
    hlast_ref[...] = h
    tail = xpad_ref[rows:rows + halo, :]
    convlast_ref[...] = tail.reshape(CONV_W - 1, bg, dr)
    xpad_ref[0:halo, :] = tail

    y = _dot((hs_ref[...] * sg_ref[...]).astype(BF16), w_out_ref[...])
    z = alpha * x + (1.0 + gate_c)[None] * y.reshape(tc, bg, d)
    x1_ref[...] = _layer_norm(z, lng_ref[...][None], lnb_ref[...][None])


def _rglru_layer(xt, mods, h0, conv0t, w_in, conv_w, conv_b, wg, b_rg, b_ig, lam, w_out, ln_g,
                 ln_b, *, alpha, tc):
    t_len, bg, d = xt.shape
    dr = w_out.shape[0]
    rows = tc * bg
    halo = (CONV_W - 1) * bg
    nblk = dr // RG_BLOCK
    kern = functools.partial(_rglru_kernel, tc=tc, bg=bg, d=d, dr=dr, alpha=alpha)
    return pl.pallas_call(
        kern,
        grid=(t_len // tc,),
        in_specs=[
            pl.BlockSpec((tc, bg, d), lambda i: (i, 0, 0)),
            _const_spec((bg, 3 * d)),
            _const_spec((bg, dr)),
            _const_spec((CONV_W - 1, bg, dr)),
            _const_spec((d, 2 * dr)),
            _const_spec((CONV_W, dr)),
            _const_spec((1, dr)),
            _const_spec((nblk, RG_BLOCK, 2 * RG_BLOCK)),
            _const_spec((1, dr)),
            _const_spec((1, dr)),
            _const_spec((1, dr)),
            _const_spec((dr, d)),
            _const_spec((1, d)),
            _const_spec((1, d)),
        ],
        out_specs=[
            pl.BlockSpec((tc, bg, d), lambda i: (i, 0, 0)),
            _const_spec((bg, dr)),
            _const_spec((CONV_W - 1, bg, dr)),
        ],
        out_shape=[
            jax.ShapeDtypeStruct((t_len, bg, d), F32),
            jax.ShapeDtypeStruct((bg, dr), F32),
            jax.ShapeDtypeStruct((CONV_W - 1, bg, dr), F32),
        ],
        scratch_shapes=[
            pltpu.VMEM((rows + halo, dr), F32),
            pltpu.VMEM((rows, dr), F32),
            pltpu.VMEM((rows, dr), F32),
            pltpu.VMEM((rows, dr), F32),
            pltpu.VMEM((rows, dr), F32),
            pltpu.VMEM((bg, dr), F32),
        ],
        compiler_params=_params("arbitrary"),
        name="rglru_layer",
    )(xt, mods, h0, conv0t, w_in, conv_w, conv_b.reshape(1, dr), wg, b_rg.reshape(1, dr),
      b_ig.reshape(1, dr), lam.reshape(1, dr), w_out, ln_g.reshape(1, d), ln_b.reshape(1, d))


def _proj_kernel(x_ref, mkv_ref, mb_ref, wkv_ref, wqg_ref,
                 k_ref, v_ref, kb_ref, vb_ref, q_ref, g_ref, *, d, da):
    x = x_ref[...]
    u = (x * (1.0 + mkv_ref[:, d:2 * d]) + mkv_ref[:, 0:d]).astype(BF16)
    kv = _dot(u, wkv_ref[...])
    k = kv[:, :da]
    v = kv[:, da:]
    k_ref[...] = k.reshape(k_ref.shape)
    v_ref[...] = v.reshape(v_ref.shape)
    kb_ref[...] = k.astype(BF16)
    vb_ref[...] = v.astype(BF16)
    u2 = (x * (1.0 + mb_ref[:, d:2 * d]) + mb_ref[:, 0:d]).astype(BF16)
    qg = _dot(u2, wqg_ref[...])
    q_ref[...] = (qg[:, :da] * (HEAD_DIM ** -0.5)).astype(BF16)
    gate = qg[:, da:]
    g_ref[...] = gate * _sigmoid(gate)


def _proj(x, mkv, mb, wkv, wqg, *, tm):
    n, d = x.shape
    da = wkv.shape[1] // 2
    groups, mrows, _ = mkv.shape
    per_group = n // groups // tm
    kern = functools.partial(_proj_kernel, d=d, da=da)
    heads = da // HEAD_DIM
    row_spec = pl.BlockSpec((tm, da), lambda g, i: (g * per_group + i, 0))
    head_spec = pl.BlockSpec((tm, heads, HEAD_DIM), lambda g, i: (g * per_group + i, 0, 0))
    return pl.pallas_call(
        kern,
        grid=(groups, per_group),
        in_specs=[
            pl.BlockSpec((tm, d), lambda g, i: (g * per_group + i, 0)),
            pl.BlockSpec((None, mrows, mkv.shape[2]), lambda g, i: (g, 0, 0)),
            pl.BlockSpec((None, mrows, mb.shape[2]), lambda g, i: (g, 0, 0)),
            _const_spec(wkv.shape),
            _const_spec(wqg.shape),
        ],
        out_specs=[head_spec] * 2 + [row_spec] * 4,
        out_shape=[
            jax.ShapeDtypeStruct((n, heads, HEAD_DIM), F32),
            jax.ShapeDtypeStruct((n, heads, HEAD_DIM), F32),
            jax.ShapeDtypeStruct((n, da), BF16),
            jax.ShapeDtypeStruct((n, da), BF16),
            jax.ShapeDtypeStruct((n, da), BF16),
            jax.ShapeDtypeStruct((n, da), F32),
        ],
        compiler_params=_params("parallel", "arbitrary"),
        name="kv_q_proj",
    )(x, mkv, mb, wkv, wqg)


def _attn_kernel(bias_ref, q_ref, k_ref, v_ref, tri_ref, o_ref, c_ref, vm_ref, *, tq):
    pair = pl.program_id(1)
    i = pl.program_id(2)
    lane = lax.broadcasted_iota(jnp.int32, (1, LANES), 1)
    lo_lanes = lane < HEAD_DIM
    nblk = vm_ref.shape[0]

    @pl.when(i == 0)
    def _():
        for j in range(nblk):
            vblk = v_ref[j * tq:(j + 1) * tq, :]
            zero = jnp.zeros_like(vblk)
            vm_ref[j, 0:tq, :] = jnp.where(lo_lanes, vblk, zero)
            vm_ref[j, tq:2 * tq, :] = jnp.where(lo_lanes, zero, vblk)

    q = q_ref[...]
    zq = jnp.zeros_like(q)
    qh = (jnp.where(lo_lanes, q, zq), jnp.where(lo_lanes, zq, q))
    bias = (bias_ref[0, 2 * pair], bias_ref[0, 2 * pair + 1])
    rel = (lax.broadcasted_iota(jnp.int32, (tq, tq), 1)
           < lax.broadcasted_iota(jnp.int32, (tq, tq), 0))

    def visit(js, causal):
        ids = [(n, hh) for n in range(len(js)) for hh in range(2)]
        kblks = [k_ref[pl.ds(pl.multiple_of(j * tq, tq), tq), :] for j in js]
        zs = [_dot_nt(qh[hh], kblks[n]) + bias[hh] for n, hh in ids]
        sps = [jnp.maximum(z, 0.0) + jnp.log(1.0 + jnp.exp2(jnp.abs(z) * (-LOG2E))) for z in zs]
        if causal is not None:
            sps = [jnp.where(causal, sp, 0.0) for sp in sps]
        local = [_dot(sp.astype(BF16), tri_ref[...]) for sp in sps]
        cs = [c_ref[0], c_ref[1]]
        ws = []
        for (j, hh), z, sp, loc in zip(ids, zs, sps, local):
            w = jnp.exp2(((z - sp) - (loc + cs[hh])) * LOG2E)
            if causal is not None:
                w = jnp.where(causal, w, 0.0)
            ws.append(w.astype(BF16))
            cs[hh] = cs[hh] + jnp.sum(sp, axis=1, keepdims=True)
        acc = None
        for n, j in enumerate(js):
            part = _dot(jnp.concatenate(ws[2 * n:2 * n + 2], axis=1), vm_ref[j])
            acc = part if acc is None else acc + part
        o_ref[...] += acc
        c_ref[0] = cs[0]
        c_ref[1] = cs[1]

    o_ref[...] = jnp.zeros_like(o_ref)
    c_ref[...] = jnp.zeros_like(c_ref)
    visit([i], rel)

    def body(s, carry):
        visit([i - 1 - 2 * s, i - 2 - 2 * s], None)
        return carry

    lax.fori_loop(0, i // 2, body, 0)

    @pl.when(i % 2 == 1)
    def _():
        visit([0], None)


def _attention(q, kb, vb, bias, *, batch, tq):
    n, da = q.shape
    t_len = n // batch
    nq = t_len // tq
    pairs = da // LANES
    tri = (lax.broadcasted_iota(jnp.int32, (tq, tq), 0)
           > lax.broadcasted_iota(jnp.int32, (tq, tq), 1)).astype(BF16)
    kern = functools.partial(_attn_kernel, tq=tq)
    return pl.pallas_call(
        kern,
        grid=(batch, pairs, nq),
        in_specs=[
            pl.BlockSpec(memory_space=pltpu.SMEM),
            pl.BlockSpec((tq, LANES), lambda b, p, i: (b * nq + i, p)),
            pl.BlockSpec((t_len, LANES), lambda b, p, i: (b, p)),
            pl.BlockSpec((t_len, LANES), lambda b, p, i: (b, p)),
            _const_spec((tq, tq)),
        ],
        out_specs=pl.BlockSpec((tq, LANES), lambda b, p, i: (b * nq + i, p)),
        out_shape=jax.ShapeDtypeStruct((n, da), F32),
        scratch_shapes=[
            pltpu.VMEM((2, tq, 1), F32),
            pltpu.VMEM((nq, 2 * tq, LANES), BF16),
        ],
        compiler_params=_params("parallel", "parallel", "arbitrary"),
        name="sb_attention",
    )(bias, q, kb, vb, tri)


def _sattn_kernel(pt_ref, q_ref, kn_ref, vn_ref, bias_ref, tri2_ref, *refs, n_new, page):
    del pt_ref
    k_refs = refs[:PAGES_PER_STEP]
    v_refs = refs[PAGES_PER_STEP:2 * PAGES_PER_STEP]
    o_ref = refs[2 * PAGES_PER_STEP]
    acc_ref, c_ref, qbd_ref, kpad_ref, vpad_ref = refs[2 * PAGES_PER_STEP + 1:]
    g = pl.program_id(1)
    ncol, da = qbd_ref.shape
    heads = da // HEAD_DIM

    def visit(kbs, vbs, valid, carry):
        zs = [_dot_nt(kb, qbd_ref[...]) + bias_ref[...] for kb in kbs]
        sps = [_softplus(z) for z in zs]
        if valid is not None:
            sps = [jnp.where(valid, sp, 0.0) for sp in sps]
        local = [_dot(tri2_ref[...], jnp.concatenate(_split_hi_lo(sp), axis=0)) for sp in sps]
        acc = None
        for z, sp, loc, vb in zip(zs, sps, local, vbs):
            w = jnp.exp((z - sp) - (loc + carry))
            if valid is not None:
                w = jnp.where(valid, w, 0.0)
            part = _dot(w.T.astype(BF16), vb)
            acc = part if acc is None else acc + part
            carry = carry + jnp.sum(sp, axis=0, keepdims=True)
        return acc, carry

    @pl.when(g == 0)
    def _():
        q = q_ref[...]
        qrep = jnp.concatenate(
            [jnp.broadcast_to(q[t:t + 1, :], (heads, da)) for t in range(n_new)], axis=0)
        own = (lax.broadcasted_iota(jnp.int32, (ncol, da), 0) % heads
               == lax.broadcasted_iota(jnp.int32, (ncol, da), 1) // HEAD_DIM)
        qbd_ref[...] = jnp.where(own, qrep, 0.0).astype(BF16)
        kpad_ref[...] = jnp.zeros_like(kpad_ref)
        vpad_ref[...] = jnp.zeros_like(vpad_ref)
        kpad_ref[0:SUBLANES, :] = kn_ref[...]
        vpad_ref[0:SUBLANES, :] = vn_ref[...]
        valid = (lax.broadcasted_iota(jnp.int32, (page, ncol), 0)
                 < lax.broadcasted_iota(jnp.int32, (page, ncol), 1) // heads)
        acc_ref[...], c_ref[...] = visit([kpad_ref[...].astype(BF16)], [vpad_ref[...].astype(BF16)],
                                         valid, jnp.zeros(c_ref.shape, F32))

    def dense(ref):
        return ref[...].astype(BF16).reshape(page, da)

    acc, carry = visit([dense(r) for r in k_refs], [dense(r) for r in v_refs], None, c_ref[...])
    acc_ref[...] += acc
    c_ref[...] = carry

    @pl.when(g == pl.num_programs(1) - 1)
    def _():
        own = (lax.broadcasted_iota(jnp.int32, (ncol, da), 0) % heads
               == lax.broadcasted_iota(jnp.int32, (ncol, da), 1) // HEAD_DIM)
        acc = jnp.where(own, acc_ref[...], 0.0)
        o_ref[...] = jnp.zeros_like(o_ref)
        for t in range(n_new):
            o_ref[t:t + 1, :] = jnp.sum(acc[t * heads:(t + 1) * heads, :], axis=0, keepdims=True)


def _sample_attention(q8, kn8, vn8, bias_cols, cache_k, cache_v, page_table, *, n_new):
    bs, _, da = q8.shape
    page = cache_k.shape[1]
    n_pages = page_table.shape[1]
    heads = da // HEAD_DIM
    ncol = n_new * heads
    steps = n_pages // PAGES_PER_STEP
    tri = (lax.broadcasted_iota(jnp.int32, (page, page), 1)
           > lax.broadcasted_iota(jnp.int32, (page, page), 0)).astype(BF16)
    tri2 = jnp.concatenate([tri, tri], axis=1)

    def page_spec(r):
        def index(b, g, pt):
            return (pt[b * n_pages + (n_pages - 1 - (g * PAGES_PER_STEP + r))], 0, 0, 0)
        return pl.BlockSpec((None, page, heads, HEAD_DIM), index)

    tok_spec = pl.BlockSpec((None, SUBLANES, da), lambda b, g, pt: (b, 0, 0))
    kern = functools.partial(_sattn_kernel, n_new=n_new, page=page)
    grid_spec = pltpu.PrefetchScalarGridSpec(
        num_scalar_prefetch=1,
        grid=(bs, steps),
        in_specs=[tok_spec, tok_spec, tok_spec,
                  pl.BlockSpec((1, ncol), lambda b, g, pt: (0, 0)),
                  pl.BlockSpec((page, 2 * page), lambda b, g, pt: (0, 0))]
                 + [page_spec(r) for r in range(PAGES_PER_STEP)] * 2,
        out_specs=tok_spec,
        scratch_shapes=[
            pltpu.VMEM((ncol, da), F32),
            pltpu.VMEM((1, ncol), F32),
            pltpu.VMEM((ncol, da), BF16),
            pltpu.VMEM((page, da), F32),
            pltpu.VMEM((page, da), F32),
        ],
    )
    return pl.pallas_call(
        kern,
        grid_spec=grid_spec,
        out_shape=jax.ShapeDtypeStruct((bs, SUBLANES, da), F32),
        compiler_params=_params("parallel", "arbitrary"),
        name="sb_attention_paged",
    )(page_table.reshape(-1), q8, kn8, vn8, bias_cols, tri2,
      *([cache_k] * PAGES_PER_STEP), *([cache_v] * PAGES_PER_STEP))


def _out_kernel(o_ref, g_ref, x_ref, gc_ref, w_ref, lng_ref, lnb_ref, y_ref, *, alpha):
    y = _dot((o_ref[...] * g_ref[...]).astype(BF16), w_ref[...])
    z = alpha * x_ref[...] + (1.0 + gc_ref[...]) * y
    y_ref[...] = _layer_norm(z, lng_ref[...], lnb_ref[...])


def _out_layer(o, g, x, gate_c, w, ln_g, ln_b, *, alpha, tm):
    n, d = x.shape
    da = o.shape[1]
    groups, mrows, _ = gate_c.shape
    per_group = n // groups // tm
    kern = functools.partial(_out_kernel, alpha=alpha)

    def rows(width):
        return pl.BlockSpec((tm, width), lambda gi, i: (gi * per_group + i, 0))

    return pl.pallas_call(
        kern,
        grid=(groups, per_group),
        in_specs=[rows(da), rows(da), rows(d),
                  pl.BlockSpec((None, mrows, d), lambda gi, i: (gi, 0, 0)),
                  _const_spec(w.shape), _const_spec((1, d)), _const_spec((1, d))],
        out_specs=rows(d),
        out_shape=jax.ShapeDtypeStruct((n, d), F32),
        compiler_params=_params("parallel", "arbitrary"),
        name="attn_out_ln",
    )(o, g, x, gate_c, w, ln_g.reshape(1, d), ln_b.reshape(1, d))


def _largest_divisor(n, cap):
    return max(k for k in range(1, cap + 1) if n % k == 0)


def kernel(x_prompt, x_sample, c_prompt, c_sample, state_lru_h, state_lru_conv, cache_k, cache_v, page_table, ada_w_a, ada_b_a, w_in_a, conv_w, conv_b, w_rg, b_rg, w_ig, b_ig, lru_lambda, w_out_a, ln_g_a, ln_b_a, ada_w_kv, ada_b_kv, w_k, w_v, ada_w_b, ada_b_b, w_in_b, sb_bias, w_out_b, ln_g_b, ln_b_b):
    assert ada_w_a.shape[0] == 1 and ada_w_b.shape[0] == 1, "one RG-LRU and one attention layer"
    bp, t_len, d = x_prompt.shape
    bs, n_new, _ = x_sample.shape
    dr = w_out_a.shape[1]
    da = w_k.shape[1]
    heads = sb_bias.shape[1]
    assert da == heads * HEAD_DIM and w_rg.shape[2] == RG_BLOCK and conv_w.shape[1] == CONV_W
    assert bp % SUBLANES == 0 and bs % SUBLANES == 0 and n_new <= SUBLANES
    alpha = (2.0 *(ada_w_a.shape[0] + ada_w_b.shape[0])) ** 0.25

    c_all = jnp.concatenate([c_prompt, c_sample], axis=0)
    mod_a = _ada(c_all, ada_w_a[0], ada_b_a[0])
    mod_kv = _ada(c_all, ada_w_kv, ada_b_kv)
    mod_b = _ada(c_all, ada_w_b[0], ada_b_b[0])

    w_in = w_in_a[0].astype(BF16)
    wg = jnp.concatenate([w_rg[0], w_ig[0]], axis=-1).astype(BF16)
    w_out = w_out_a[0].astype(BF16)
    wkv = jnp.concatenate([w_k, w_v], axis=1).astype(BF16)
    wqg = w_in_b[0].astype(BF16)
    w_ob = w_out_b[0].astype(BF16)
    rg_args = (w_in, conv_w[0], conv_b[0], wg, b_rg[0], b_ig[0], lru_lambda[0], w_out,
               ln_g_a[0], ln_b_a[0])

    tc = _largest_divisor(t_len, max(1, 512 // bp))
    x1t, h_p, conv_p = _rglru_layer(
        x_prompt.transpose(1, 0, 2), mod_a[:bp], jnp.zeros((bp, dr), F32),
        jnp.zeros((CONV_W - 1, bp, dr), F32), *rg_args, alpha=alpha, tc=tc)
    x1 = x1t.transpose(1, 0, 2).reshape(bp * t_len, d)
    tm = _largest_divisor(t_len, 512)
    k_p, v_p, kb, vb, q, g = _proj(
        x1, mod_kv[:bp, None, :], mod_b[:bp, None, :2 * d], wkv, wqg, tm=tm)
    tq = _largest_divisor(t_len, 256)
    o = _attention(q, kb, vb, sb_bias, batch=bp, tq=tq)
    y_p = _out_layer(o, g, x1, mod_b[:bp, None, 2 * d:], w_ob, ln_g_b[0], ln_b_b[0],
                     alpha=alpha, tm=tm)

    ns = n_new * bs
    x1s_t, h_s, conv_s = _rglru_layer(
        x_sample.transpose(1, 0, 2), mod_a[bp:], state_lru_h[0],
        state_lru_conv[0].transpose(1, 0, 2), *rg_args, alpha=alpha, tc=n_new)
    x1s = x1s_t.reshape(ns, d)
    mkv_s = jnp.tile(mod_kv[bp:], (n_new, 1))[None]
    mb_s = jnp.tile(mod_b[bp:], (n_new, 1))[None]
    k_s, v_s, _, _, q_s, g_s = _proj(x1s, mkv_s, mb_s[:, :, :2 * d], wkv, wqg, tm=ns)

    def per_sequence(a):
        a = a.astype(F32).reshape(n_new, bs, da).transpose(1, 0, 2)
        return jnp.pad(a, ((0, 0), (0, SUBLANES - n_new), (0, 0)))

    bias_cols = jnp.tile(sb_bias[0], n_new).reshape(1, n_new * heads)
    o_s8 = _sample_attention(
        per_sequence(q_s), per_sequence(k_s), per_sequence(v_s), bias_cols,
        cache_k, cache_v, page_table,
        n_new=n_new)
    o_s = o_s8[:, :n_new].transpose(1, 0, 2).reshape(ns, da)
    y_s = _out_layer(o_s, g_s, x1s, mb_s[:, :, 2 * d:], w_ob, ln_g_b[0], ln_b_b[0],
                     alpha=alpha, tm=ns)

    def seq_major(a, *tail):
        return a.reshape(n_new, bs, *tail).transpose(1, 0, *range(2, 2 + len(tail)))

    return (
        y_p.reshape(bp, t_len, d),
        seq_major(y_s, d),
        h_p[None],
        conv_p.transpose(1, 0, 2)[None],
        k_p.reshape(bp, t_len, heads, HEAD_DIM),
        v_p.reshape(bp, t_len, heads, HEAD_DIM),
        h_s[None],
        conv_s.transpose(1, 0, 2)[None],
        seq_major(k_s, heads, HEAD_DIM),
        seq_major(v_s, heads, HEAD_DIM),
    )
```

```python
import functools
import math

import jax
import jax.numpy as jnp
from jax import lax
from jax.experimental import pallas as pl
from jax.experimental.pallas import tpu as pltpu

HEAD_DIM = 64
RG_BLOCK = 128
RG_C = 8.0
CONV_W = 4
LN_EPS = 1e-5
LOG2E = math.log2(math.e)
LANES = 128
SUBLANES = 8
PAGES_PER_STEP = 8
VMEM_LIMIT = 56 * 1024 * 1024

BF16 = jnp.bfloat16
F32 = jnp.float32


def _dot(a, b):
    return jnp.dot(a, b, preferred_element_type=F32)


def _dot_nt(a, b):
    return lax.dot_general(a, b, (((1,), (1,)), ((), ())), preferred_element_type=F32)


def _sigmoid(x):
    return 1.0 / (1.0 + jnp.exp(-x))


def _softplus(x):
    return jnp.maximum(x, 0.0) + jnp.log(1.0 + jnp.exp(-jnp.abs(x)))


def _layer_norm(z, g, b):
    mu = jnp.mean(z, axis=-1, keepdims=True)
    zc = z - mu
    var = jnp.mean(zc * zc, axis=-1, keepdims=True)
    return zc * lax.rsqrt(var + LN_EPS) * g + b


def _split_hi_lo(x):
    hi = x.astype(BF16)
    lo = (x - hi.astype(F32)).astype(BF16)
    return hi, lo


def _params(*sem):
    return pltpu.CompilerParams(dimension_semantics=sem, vmem_limit_bytes=VMEM_LIMIT)


def _const_spec(shape):
    zeros = (0,) * len(shape)
    return pl.BlockSpec(shape, lambda *_: zeros)


def _ada_kernel(c_ref, w_ref, b_ref, o_ref):
    c = c_ref[...]
    s = (c * _sigmoid(c)).astype(BF16)
    o_ref[...] = _dot(s, w_ref[...].astype(BF16)) + b_ref[...]


def _ada(c, w, b):
    n, d = c.shape
    width = w.shape[1]
    tn = 1024 if width % 1024 == 0 else width
    return pl.pallas_call(
        _ada_kernel,
        grid=(width // tn,),
        in_specs=[
            pl.BlockSpec((n, d), lambda j: (0, 0)),
            pl.BlockSpec((d, tn), lambda j: (0, j)),
            pl.BlockSpec((1, tn), lambda j: (0, j)),
        ],
        out_specs=pl.BlockSpec((n, tn), lambda j: (0, j)),
        out_shape=jax.ShapeDtypeStruct((n, width), F32),
        compiler_params=_params("arbitrary"),
        name="ada_mod",
    )(c, w, b.reshape(1, width))


def _rglru_kernel(x_ref, mod_ref, h0_ref, conv0_ref, w_in_ref, convw_ref, convb_ref, wg_ref,
                  brg_ref, big_ref, lam_ref, w_out_ref, lng_ref, lnb_ref,
                  x1_ref, hlast_ref, convlast_ref,
                  xpad_ref, sg_ref, a_ref, b_ref, hs_ref, h_ref, *, tc, bg, d, dr, alpha):
    rows = tc * bg
    halo = (CONV_W - 1) * bg

    @pl.when(pl.program_id(0) == 0)
    def _():
        h_ref[...] = h0_ref[...]
        xpad_ref[0:halo, :] = conv0_ref[...].reshape(halo, dr)

    shift = mod_ref[:, 0:d]
    scale = mod_ref[:, d:2 * d]
    gate_c = mod_ref[:, 2 * d:3 * d]
    x = x_ref[...]
    u = x * (1.0 + scale)[None] + shift[None]
    xg = _dot(u.reshape(rows, d).astype(BF16), w_in_ref[...])
    gate = xg[:, dr:]
    sg_ref[...] = gate * _sigmoid(gate)
    xpad_ref[halo:halo + rows, :] = xg[:, :dr]

    lam = lam_ref[...]
    log_sig_lam = -_softplus(-lam)
    for n in range(dr // RG_BLOCK):
        sl = slice(n * RG_BLOCK, (n + 1) * RG_BLOCK)
        xc = convb_ref[:, sl] + sum(
            xpad_ref[k * bg:k * bg + rows, sl] * convw_ref[k:k + 1, sl] for k in range(CONV_W))
        pre = _dot(xc.astype(BF16), wg_ref[n])
        r = _sigmoid(pre[:, :RG_BLOCK] + brg_ref[:, sl])
        ig = _sigmoid(pre[:, RG_BLOCK:] + big_ref[:, sl])
        log_a = (RG_C * r) * log_sig_lam[:, sl]
        a = jnp.exp(log_a)
        a_ref[:, sl] = a
        b_ref[:, sl] = jnp.sqrt(1.0 - a * a) * (ig * xc)

    h = h_ref[...]
    for t in range(tc):
        h = a_ref[t * bg:(t + 1) * bg, :] * h + b_ref[t * bg:(t + 1) * bg, :]
        hs_ref[t * bg:(t + 1) * bg, :] = h
    h_ref[...] = h
    hlast_ref[...] = h
    tail = xpad_ref[rows:rows + halo, :]
    convlast_ref[...] = tail.reshape(CONV_W - 1, bg, dr)
    xpad_ref[0:halo, :] = tail

    y = _dot((hs_ref[...] * sg_ref[...]).astype(BF16), w_out_ref[...])
    z = alpha * x + (1.0 + gate_c)[None] * y.reshape(tc, bg, d)
    x1_ref[...] = _layer_norm(z, lng_ref[...][None], lnb_ref[...][None])


def _rglru_layer(xt, mods, h0, conv0t, w_in, conv_w, conv_b, wg, b_rg, b_ig, lam, w_out, ln_g,
                 ln_b, *, alpha, tc):
    t_len, bg, d = xt.shape
    dr = w_out.shape[0]
    rows = tc * bg
    halo = (CONV_W - 1) * bg
    nblk = dr // RG_BLOCK
    kern = functools.partial(_rglru_kernel, tc=tc, bg=bg, d=d, dr=dr, alpha=alpha)
    return pl.pallas_call(
        kern,
        grid=(t_len // tc,),
        in_specs=[
            pl.BlockSpec((tc, bg, d), lambda i: (i, 0, 0)),
            _const_spec((bg, 3 * d)),
            _const_spec((bg, dr)),
            _const_spec((CONV_W - 1, bg, dr)),
            _const_spec((d, 2 * dr)),
            _const_spec((CONV_W, dr)),
            _const_spec((1, dr)),
            _const_spec((nblk, RG_BLOCK, 2 * RG_BLOCK)),
            _const_spec((1, dr)),
            _const_spec((1, dr)),
            _const_spec((1, dr)),
            _const_spec((dr, d)),
            _const_spec((1, d)),
            _const_spec((1, d)),
        ],
        out_specs=[
            pl.BlockSpec((tc, bg, d), lambda i: (i, 0, 0)),
            _const_spec((bg, dr)),
            _const_spec((CONV_W - 1, bg, dr)),
        ],
        out_shape=[
            jax.ShapeDtypeStruct((t_len, bg, d), F32),
            jax.ShapeDtypeStruct((bg, dr), F32),
            jax.ShapeDtypeStruct((CONV_W - 1, bg, dr), F32),
        ],
        scratch_shapes=[
            pltpu.VMEM((rows + halo, dr), F32),
            pltpu.VMEM((rows, dr), F32),
            pltpu.VMEM((rows, dr), F32),
            pltpu.VMEM((rows, dr), F32),
            pltpu.VMEM((rows, dr), F32),
            pltpu.VMEM((bg, dr), F32),
        ],
        compiler_params=_params("arbitrary"),
        name="rglru_layer",
    )(xt, mods, h0, conv0t, w_in, conv_w, conv_b.reshape(1, dr), wg, b_rg.reshape(1, dr),
      b_ig.reshape(1, dr), lam.reshape(1, dr), w_out, ln_g.reshape(1, d), ln_b.reshape(1, d))


def _proj_kernel(x_ref, mkv_ref, mb_ref, wkv_ref, wqg_ref,
                 k_ref, v_ref, kb_ref, vb_ref, q_ref, g_ref, *, d, da):
    x = x_ref[...]
    u = (x * (1.0 + mkv_ref[:, d:2 * d]) + mkv_ref[:, 0:d]).astype(BF16)
    kv = _dot(u, wkv_ref[...])
    k = kv[:, :da]
    v = kv[:, da:]
    k_ref[...] = k.reshape(k_ref.shape)
    v_ref[...] = v.reshape(v_ref.shape)
    kb_ref[...] = k.astype(BF16)
    vb_ref[...] = v.astype(BF16)
    u2 = (x * (1.0 + mb_ref[:, d:2 * d]) + mb_ref[:, 0:d]).astype(BF16)
    qg = _dot(u2, wqg_ref[...])
    q_ref[...] = (qg[:, :da] * (HEAD_DIM ** -0.5)).astype(BF16)
    gate = qg[:, da:]
    g_ref[...] = gate * _sigmoid(gate)


def _proj(x, mkv, mb, wkv, wqg, *, tm):
    n, d = x.shape
    da = wkv.shape[1] // 2
    groups, mrows, _ = mkv.shape
    per_group = n // groups // tm
    kern = functools.partial(_proj_kernel, d=d, da=da)
    heads = da // HEAD_DIM
    row_spec = pl.BlockSpec((tm, da), lambda g, i: (g * per_group + i, 0))
    head_spec = pl.BlockSpec((tm, heads, HEAD_DIM), lambda g, i: (g * per_group + i, 0, 0))
    return pl.pallas_call(
        kern,
        grid=(groups, per_group),
        in_specs=[
            pl.BlockSpec((tm, d), lambda g, i: (g * per_group + i, 0)),
            pl.BlockSpec((None, mrows, mkv.shape[2]), lambda g, i: (g, 0, 0)),
            pl.BlockSpec((None, mrows, mb.shape[2]), lambda g, i: (g, 0, 0)),
            _const_spec(wkv.shape),
            _const_spec(wqg.shape),
        ],
        out_specs=[head_spec] * 2 + [row_spec] * 4,
        out_shape=[
            jax.ShapeDtypeStruct((n, heads, HEAD_DIM), F32),
            jax.ShapeDtypeStruct((n, heads, HEAD_DIM), F32),
            jax.ShapeDtypeStruct((n, da), BF16),
            jax.ShapeDtypeStruct((n, da), BF16),
            jax.ShapeDtypeStruct((n, da), BF16),
            jax.ShapeDtypeStruct((n, da), F32),
        ],
        compiler_params=_params("parallel", "arbitrary"),
        name="kv_q_proj",
    )(x, mkv, mb, wkv, wqg)


def _attn_kernel(bias_ref, q_ref, k_ref, v_ref, tri_ref, o_ref, c_ref, vm_ref, *, tq):
    pair = pl.program_id(1)
    i = pl.program_id(2)
    lane = lax.broadcasted_iota(jnp.int32, (1, LANES), 1)
    lo_lanes = lane < HEAD_DIM
    nblk = vm_ref.shape[0]

    @pl.when(i == 0)
    def _():
        for j in range(nblk):
            vblk = v_ref[j * tq:(j + 1) * tq, :]
            zero = jnp.zeros_like(vblk)
            vm_ref[j, 0:tq, :] = jnp.where(lo_lanes, vblk, zero)
            vm_ref[j, tq:2 * tq, :] = jnp.where(lo_lanes, zero, vblk)

    q = q_ref[...]
    zq = jnp.zeros_like(q)
    qh = (jnp.where(lo_lanes, q, zq), jnp.where(lo_lanes, zq, q))
    bias = (bias_ref[0, 2 * pair], bias_ref[0, 2 * pair + 1])
    rel = (lax.broadcasted_iota(jnp.int32, (tq, tq), 1)
           < lax.broadcasted_iota(jnp.int32, (tq, tq), 0))

    def visit(js, causal):
        ids = [(n, hh) for n in range(len(js)) for hh in range(2)]
        kblks = [k_ref[pl.ds(pl.multiple_of(j * tq, tq), tq), :] for j in js]
        zs = [_dot_nt(qh[hh], kblks[n]) + bias[hh] for n, hh in ids]
        sps = [jnp.maximum(z, 0.0) + jnp.log(1.0 + jnp.exp2(jnp.abs(z) * (-LOG2E))) for z in zs]
        if causal is not None:
            sps = [jnp.where(causal, sp, 0.0) for sp in sps]
        local = [_dot(sp.astype(BF16), tri_ref[...]) for sp in sps]
        cs = [c_ref[0], c_ref[1]]
        ws = []
        for (j, hh), z, sp, loc in zip(ids, zs, sps, local):
            w = jnp.exp2(((z - sp) - (loc + cs[hh])) * LOG2E)
            if causal is not None:
                w = jnp.where(causal, w, 0.0)
            ws.append(w.astype(BF16))
            cs[hh] = cs[hh] + jnp.sum(sp, axis=1, keepdims=True)
        acc = None
        for n, j in enumerate(js):
            part = _dot(jnp.concatenate(ws[2 * n:2 * n + 2], axis=1), vm_ref[j])
            acc = part if acc is None else acc + part
        o_ref[...] += acc
        c_ref[0] = cs[0]
        c_ref[1] = cs[1]

    o_ref[...] = jnp.zeros_like(o_ref)
    c_ref[...] = jnp.zeros_like(c_ref)
    visit([i], rel)

    def body(s, carry):
        visit([i - 1 - 2 * s, i - 2 - 2 * s], None)
        return carry

    lax.fori_loop(0, i // 2, body, 0)

    @pl.when(i % 2 == 1)
    def _():
        visit([0], None)


def _attention(q, kb, vb, bias, *, batch, tq):
    n, da = q.shape
    t_len = n // batch
    nq = t_len // tq
    pairs = da // LANES
    tri = (lax.broadcasted_iota(jnp.int32, (tq, tq), 0)
           > lax.broadcasted_iota(jnp.int32, (tq, tq), 1)).astype(BF16)
    kern = functools.partial(_attn_kernel, tq=tq)
    return pl.pallas_call(
        kern,
        grid=(batch, pairs, nq),
        in_specs=[
            pl.BlockSpec(memory_space=pltpu.SMEM),
            pl.BlockSpec((tq, LANES), lambda b, p, i: (b * nq + i, p)),
            pl.BlockSpec((t_len, LANES), lambda b, p, i: (b, p)),
            pl.BlockSpec((t_len, LANES), lambda b, p, i: (b, p)),
            _const_spec((tq, tq)),
        ],
        out_specs=pl.BlockSpec((tq, LANES), lambda b, p, i: (b * nq + i, p)),
        out_shape=jax.ShapeDtypeStruct((n, da), F32),
        scratch_shapes=[
            pltpu.VMEM((2, tq, 1), F32),
            pltpu.VMEM((nq, 2 * tq, LANES), BF16),
        ],
        compiler_params=_params("parallel", "parallel", "arbitrary"),
        name="sb_attention",
    )(bias, q, kb, vb, tri)


def _sattn_kernel(pt_ref, q_ref, kn_ref, vn_ref, bias_ref, tri2_ref, *refs, n_new, page):
    del pt_ref
    k_refs = refs[:PAGES_PER_STEP]
    v_refs = refs[PAGES_PER_STEP:2 * PAGES_PER_STEP]
    o_ref = refs[2 * PAGES_PER_STEP]
    acc_ref, c_ref, qbd_ref, kpad_ref, vpad_ref = refs[2 * PAGES_PER_STEP + 1:]
    g = pl.program_id(1)
    nrow, da = qbd_ref.shape
    heads = da // HEAD_DIM

    def visit(score, weigh, n, valid, carry):
        zs = [score(i) + bias_ref[...] for i in range(n)]
        sps = [_softplus(z) for z in zs]
        if valid is not None:
            sps = [jnp.where(valid, sp, 0.0) for sp in sps]
        local = [_dot(jnp.concatenate(_split_hi_lo(sp), axis=1), tri2_ref[...]) for sp in sps]
        acc = None
        for i in range(n):
            w = jnp.exp((zs[i] - sps[i]) - (local[i] + carry))
            if valid is not None:
                w = jnp.where(valid, w, 0.0)
            part = weigh(i, w.astype(BF16))
            acc = part if acc is None else acc + part
            carry = carry + jnp.sum(sps[i], axis=1, keepdims=True)
        return acc, carry

    @pl.when(g == 0)
    def _():
        q = q_ref[...]
        qrep = jnp.concatenate(
            [jnp.broadcast_to(q[t:t + 1, :], (heads, da)) for t in range(n_new)], axis=0)
        own = (lax.broadcasted_iota(jnp.int32, (nrow, da), 0) % heads
               == lax.broadcasted_iota(jnp.int32, (nrow, da), 1) // HEAD_DIM)
        qbd_ref[...] = jnp.where(own, qrep, 0.0).astype(BF16)
        kpad_ref[...] = jnp.zeros_like(kpad_ref)
        vpad_ref[...] = jnp.zeros_like(vpad_ref)
        kpad_ref[0:SUBLANES, :] = kn_ref[...]
        vpad_ref[0:SUBLANES, :] = vn_ref[...]
        valid = (lax.broadcasted_iota(jnp.int32, (nrow, page), 1)
                 < lax.broadcasted_iota(jnp.int32, (nrow, page), 0) // heads)
        acc_ref[...], c_ref[...] = visit(
            lambda i: _dot_nt(qbd_ref[...], kpad_ref[...].astype(BF16)),
            lambda i, w: _dot(w, vpad_ref[...].astype(BF16)),
            1, valid, jnp.zeros(c_ref.shape, F32))

    def flat(ref):
        return ref[...].reshape(da, page).astype(BF16)

    acc, carry = visit(
        lambda i: _dot(qbd_ref[...], flat(k_refs[i])),
        lambda i, w: _dot_nt(w, flat(v_refs[i])),
        PAGES_PER_STEP, None, c_ref[...])
    acc_ref[...] += acc
    c_ref[...] = carry

    @pl.when(g == pl.num_programs(1) - 1)
    def _():
        own = (lax.broadcasted_iota(jnp.int32, (nrow, da), 0) % heads
               == lax.broadcasted_iota(jnp.int32, (nrow, da), 1) // HEAD_DIM)
        acc = jnp.where(own, acc_ref[...], 0.0)
        o_ref[...] = jnp.zeros_like(o_ref)
        for t in range(n_new):
            o_ref[t:t + 1, :] = jnp.sum(acc[t * heads:(t + 1) * heads, :], axis=0, keepdims=True)


def _sample_attention(q8, kn8, vn8, bias_rows, cache_kt, cache_vt, page_table, *, n_new):
    bs, _, da = q8.shape
    heads, _, page = cache_kt.shape[1:]
    n_pages = page_table.shape[1]
    nrow = n_new * heads
    steps = n_pages // PAGES_PER_STEP
    tri = (lax.broadcasted_iota(jnp.int32, (page, page), 0)
           > lax.broadcasted_iota(jnp.int32, (page, page), 1)).astype(BF16)
    tri2 = jnp.concatenate([tri, tri], axis=0)

    def page_spec(r):
        def index(b, g, pt):
            return (pt[b * n_pages + (n_pages - 1 - (g * PAGES_PER_STEP + r))], 0, 0, 0)
        return pl.BlockSpec((None, heads, HEAD_DIM, page), index)

    tok_spec = pl.BlockSpec((None, SUBLANES, da), lambda b, g, pt: (b, 0, 0))
    kern = functools.partial(_sattn_kernel, n_new=n_new, page=page)
    grid_spec = pltpu.PrefetchScalarGridSpec(
        num_scalar_prefetch=1,
        grid=(bs, steps),
        in_specs=[tok_spec, tok_spec, tok_spec,
                  pl.BlockSpec((nrow, 1), lambda b, g, pt: (0, 0)),
                  pl.BlockSpec((2 * page, page), lambda b, g, pt: (0, 0))]
                 + [page_spec(r) for r in range(PAGES_PER_STEP)] * 2,
        out_specs=tok_spec,
        scratch_shapes=[
            pltpu.VMEM((nrow, da), F32),
            pltpu.VMEM((nrow, 1), F32),
            pltpu.VMEM((nrow, da), BF16),
            pltpu.VMEM((page, da), F32),
            pltpu.VMEM((page, da), F32),
        ],
    )
    return pl.pallas_call(
        kern,
        grid_spec=grid_spec,
        out_shape=jax.ShapeDtypeStruct((bs, SUBLANES, da), F32),
        compiler_params=_params("parallel", "arbitrary"),
        name="sb_attention_paged",
    )(page_table.reshape(-1), q8, kn8, vn8, bias_rows, tri2,
      *([cache_kt] * PAGES_PER_STEP), *([cache_vt] * PAGES_PER_STEP))


def _out_kernel(o_ref, g_ref, x_ref, gc_ref, w_ref, lng_ref, lnb_ref, y_ref, *, alpha):
    y = _dot((o_ref[...] * g_ref[...]).astype(BF16), w_ref[...])
    z = alpha * x_ref[...] + (1.0 + gc_ref[...]) * y
    y_ref[...] = _layer_norm(z, lng_ref[...], lnb_ref[...])


def _out_layer(o, g, x, gate_c, w, ln_g, ln_b, *, alpha, tm):
    n, d = x.shape
    da = o.shape[1]
    groups, mrows, _ = gate_c.shape
    per_group = n // groups // tm
    kern = functools.partial(_out_kernel, alpha=alpha)

    def rows(width):
        return pl.BlockSpec((tm, width), lambda gi, i: (gi * per_group + i, 0))

    return pl.pallas_call(
        kern,
        grid=(groups, per_group),
        in_specs=[rows(da), rows(da), rows(d),
                  pl.BlockSpec((None, mrows, d), lambda gi, i: (gi, 0, 0)),
                  _const_spec(w.shape), _const_spec((1, d)), _const_spec((1, d))],
        out_specs=rows(d),
        out_shape=jax.ShapeDtypeStruct((n, d), F32),
        compiler_params=_params("parallel", "arbitrary"),
        name="attn_out_ln",
    )(o, g, x, gate_c, w, ln_g.reshape(1, d), ln_b.reshape(1, d))


def _largest_divisor(n, cap):
    return max(k for k in range(1, cap + 1) if n % k == 0)


def kernel(x_prompt, x_sample, c_prompt, c_sample, state_lru_h, state_lru_conv, cache_k, cache_v, page_table, ada_w_a, ada_b_a, w_in_a, conv_w, conv_b, w_rg, b_rg, w_ig, b_ig, lru_lambda, w_out_a, ln_g_a, ln_b_a, ada_w_kv, ada_b_kv, w_k, w_v, ada_w_b, ada_b_b, w_in_b, sb_bias, w_out_b, ln_g_b, ln_b_b):
    assert ada_w_a.shape[0] == 1 and ada_w_b.shape[0] == 1, "one RG-LRU and one attention layer"
    bp, t_len, d = x_prompt.shape
    bs, n_new, _ = x_sample.shape
    dr = w_out_a.shape[1]
    da = w_k.shape[1]
    heads = sb_bias.shape[1]
    assert da == heads * HEAD_DIM and w_rg.shape[2] == RG_BLOCK and conv_w.shape[1] == CONV_W
    assert bp % SUBLANES == 0 and bs % SUBLANES == 0 and n_new <= SUBLANES
    alpha = (2.0 *(ada_w_a.shape[0] + ada_w_b.shape[0])) ** 0.25

    c_all = jnp.concatenate([c_prompt, c_sample], axis=0)
    mod_a = _ada(c_all, ada_w_a[0], ada_b_a[0])
    mod_kv = _ada(c_all, ada_w_kv, ada_b_kv)
    mod_b = _ada(c_all, ada_w_b[0], ada_b_b[0])

    w_in = w_in_a[0].astype(BF16)
    wg = jnp.concatenate([w_rg[0], w_ig[0]], axis=-1).astype(BF16)
    w_out = w_out_a[0].astype(BF16)
    wkv = jnp.concatenate([w_k, w_v], axis=1).astype(BF16)
    wqg = w_in_b[0].astype(BF16)
    w_ob = w_out_b[0].astype(BF16)
    rg_args = (w_in, conv_w[0], conv_b[0], wg, b_rg[0], b_ig[0], lru_lambda[0], w_out,
               ln_g_a[0], ln_b_a[0])

    tc = _largest_divisor(t_len, max(1, 512 // bp))
    x1t, h_p, conv_p = _rglru_layer(
        x_prompt.transpose(1, 0, 2), mod_a[:bp], jnp.zeros((bp, dr), F32),
        jnp.zeros((CONV_W - 1, bp, dr), F32), *rg_args, alpha=alpha, tc=tc)
    x1 = x1t.transpose(1, 0, 2).reshape(bp * t_len, d)
    tm = _largest_divisor(t_len, 512)
    k_p, v_p, kb, vb, q, g = _proj(
        x1, mod_kv[:bp, None, :], mod_b[:bp, None, :2 * d], wkv, wqg, tm=tm)
    tq = _largest_divisor(t_len, 256)
    o = _attention(q, kb, vb, sb_bias, batch=bp, tq=tq)
    y_p = _out_layer(o, g, x1, mod_b[:bp, None, 2 * d:], w_ob, ln_g_b[0], ln_b_b[0],
                     alpha=alpha, tm=tm)

    ns = n_new * bs
    x1s_t, h_s, conv_s = _rglru_layer(
        x_sample.transpose(1, 0, 2), mod_a[bp:], state_lru_h[0],
        state_lru_conv[0].transpose(1, 0, 2), *rg_args, alpha=alpha, tc=n_new)
    x1s = x1s_t.reshape(ns, d)
    mkv_s = jnp.tile(mod_kv[bp:], (n_new, 1))[None]
    mb_s = jnp.tile(mod_b[bp:], (n_new, 1))[None]
    k_s, v_s, _, _, q_s, g_s = _proj(x1s, mkv_s, mb_s[:, :, :2 * d], wkv, wqg, tm=ns)

    def per_sequence(a):
        a = a.astype(F32).reshape(n_new, bs, da).transpose(1, 0, 2)
        return jnp.pad(a, ((0, 0), (0, SUBLANES - n_new), (0, 0)))

    bias_rows = jnp.tile(sb_bias[0], n_new).reshape(n_new * heads, 1)
    o_s8 = _sample_attention(
        per_sequence(q_s), per_sequence(k_s), per_sequence(v_s), bias_rows,
        cache_k.transpose(0, 2, 3, 1), cache_v.transpose(0, 2, 3, 1), page_table,
        n_new=n_new)
    o_s = o_s8[:, :n_new].transpose(1, 0, 2).reshape(ns, da)
    y_s = _out_layer(o_s, g_s, x1s, mb_s[:, :, 2 * d:], w_ob, ln_g_b[0], ln_b_b[0],
                     alpha=alpha, tm=ns)

    def seq_major(a, *tail):
        return a.reshape(n_new, bs, *tail).transpose(1, 0, *range(2, 2 + len(tail)))

    return (
        y_p.reshape(bp, t_len, d),
        seq_major(y_s, d),
        h_p[None],
        conv_p.transpose(1, 0, 2)[None],
        k_p.reshape(bp, t_len, heads, HEAD_DIM),
        v_p.reshape(bp, t_len, heads, HEAD_DIM),
        h_s[None],
        conv_s.transpose(1, 0, 2)[None],
        seq_major(k_s, heads, HEAD_DIM),
        seq_major(v_s, heads, HEAD_DIM),
    )
```

```python
import functools
import math

import jax
import jax.numpy as jnp
from jax import lax
from jax.experimental import pallas as pl
from jax.experimental.pallas import tpu as pltpu

HEAD_DIM = 64
RG_BLOCK = 128
RG_C = 8.0
CONV_W = 4
LN_EPS = 1e-5
LOG2E = math.log2(math.e)
LANES = 128
SUBLANES = 8
PAGES_PER_STEP = 16
VMEM_LIMIT = 56 * 1024 * 1024

BF16 = jnp.bfloat16
F32 = jnp.float32


def _dot(a, b):
    return jnp.dot(a, b, preferred_element_type=F32)


def _dot_nt(a, b):
    return lax.dot_general(a, b, (((1,), (1,)), ((), ())), preferred_element_type=F32)


def _sigmoid(x):
    return 1.0 / (1.0 + jnp.exp(-x))


def _softplus(x):
    return jnp.maximum(x, 0.0) + jnp.log(1.0 + jnp.exp(-jnp.abs(x)))


def _layer_norm(z, g, b):
    mu = jnp.mean(z, axis=-1, keepdims=True)
    zc = z - mu
    var = jnp.mean(zc * zc, axis=-1, keepdims=True)
    return zc * lax.rsqrt(var + LN_EPS) * g + b


def _split_hi_lo(x):
    hi = x.astype(BF16)
    lo = (x - hi.astype(F32)).astype(BF16)
    return hi, lo


def _params(*sem):
    return pltpu.CompilerParams(dimension_semantics=sem, vmem_limit_bytes=VMEM_LIMIT)


def _const_spec(shape):
    zeros = (0,) * len(shape)
    return pl.BlockSpec(shape, lambda *_: zeros)


def _ada_kernel(c_ref, w_ref, b_ref, o_ref):
    c = c_ref[...]
    s = (c * _sigmoid(c)).astype(BF16)
    o_ref[...] = _dot(s, w_ref[...].astype(BF16)) + b_ref[...]


def _ada(c, w, b):
    n, d = c.shape
    width = w.shape[1]
    tn = 1024 if width % 1024 == 0 else width
    return pl.pallas_call(
        _ada_kernel,
        grid=(width // tn,),
        in_specs=[
            pl.BlockSpec((n, d), lambda j: (0, 0)),
            pl.BlockSpec((d, tn), lambda j: (0, j)),
            pl.BlockSpec((1, tn), lambda j: (0, j)),
        ],
        out_specs=pl.BlockSpec((n, tn), lambda j: (0, j)),
        out_shape=jax.ShapeDtypeStruct((n, width), F32),
        compiler_params=_params("arbitrary"),
        name="ada_mod",
    )(c, w, b.reshape(1, width))


def _rglru_kernel(x_ref, mod_ref, h0_ref, conv0_ref, w_in_ref, convw_ref, convb_ref, wg_ref,
                  brg_ref, big_ref, lam_ref, w_out_ref, lng_ref, lnb_ref,
                  x1_ref, hlast_ref, convlast_ref,
                  xpad_ref, sg_ref, a_ref, b_ref, hs_ref, h_ref, *, tc, bg, d, dr, alpha):
    rows = tc * bg
    halo = (CONV_W - 1) * bg

    @pl.when(pl.program_id(0) == 0)
    def _():
        h_ref[...] = h0_ref[...]
        xpad_ref[0:halo, :] = conv0_ref[...].reshape(halo, dr)

    shift = mod_ref[:, 0:d]
    scale = mod_ref[:, d:2 * d]
    gate_c = mod_ref[:, 2 * d:3 * d]
    x = x_ref[...]
    u = x * (1.0 + scale)[None] + shift[None]
    xg = _dot(u.reshape(rows, d).astype(BF16), w_in_ref[...])
    gate = xg[:, dr:]
    sg_ref[...] = gate * _sigmoid(gate)
    xpad_ref[halo:halo + rows, :] = xg[:, :dr]

    lam = lam_ref[...]
    log_sig_lam = -_softplus(-lam)
    for n in range(dr // RG_BLOCK):
        sl = slice(n * RG_BLOCK, (n + 1) * RG_BLOCK)
        xc = convb_ref[:, sl] + sum(
            xpad_ref[k * bg:k * bg + rows, sl] * convw_ref[k:k + 1, sl] for k in range(CONV_W))
        pre = _dot(xc.astype(BF16), wg_ref[n])
        r = _sigmoid(pre[:, :RG_BLOCK] + brg_ref[:, sl])
        ig = _sigmoid(pre[:, RG_BLOCK:] + big_ref[:, sl])
        log_a = (RG_C * r) * log_sig_lam[:, sl]
        a = jnp.exp(log_a)
        a_ref[:, sl] = a
        b_ref[:, sl] = jnp.sqrt(1.0 - a * a) * (ig * xc)

    h = h_ref[...]
    for t in range(tc):
        h = a_ref[t * bg:(t + 1) * bg, :] * h + b_ref[t * bg:(t + 1) * bg, :]
        hs_ref[t * bg:(t + 1) * bg, :] = h
    h_ref[...] = h
    hlast_ref[...] = h
    tail = xpad_ref[rows:rows + halo, :]
    convlast_ref[...] = tail.reshape(CONV_W - 1, bg, dr)
    xpad_ref[0:halo, :] = tail

    y = _dot((hs_ref[...] * sg_ref[...]).astype(BF16), w_out_ref[...])
    z = alpha * x + (1.0 + gate_c)[None] * y.reshape(tc, bg, d)
    x1_ref[...] = _layer_norm(z, lng_ref[...][None], lnb_ref[...][None])


def _rglru_layer(xt, mods, h0, conv0t, w_in, conv_w, conv_b, wg, b_rg, b_ig, lam, w_out, ln_g,
                 ln_b, *, alpha, tc):
    t_len, bg, d = xt.shape
    dr = w_out.shape[0]
    rows = tc * bg
    halo = (CONV_W - 1) * bg
    nblk = dr // RG_BLOCK
    kern = functools.partial(_rglru_kernel, tc=tc, bg=bg, d=d, dr=dr, alpha=alpha)
    return pl.pallas_call(
        kern,
        grid=(t_len // tc,),
        in_specs=[
            pl.BlockSpec((tc, bg, d), lambda i: (i, 0, 0)),
            _const_spec((bg, 3 * d)),
            _const_spec((bg, dr)),
            _const_spec((CONV_W - 1, bg, dr)),
            _const_spec((d, 2 * dr)),
            _const_spec((CONV_W, dr)),
            _const_spec((1, dr)),
            _const_spec((nblk, RG_BLOCK, 2 * RG_BLOCK)),
            _const_spec((1, dr)),
            _const_spec((1, dr)),
            _const_spec((1, dr)),
            _const_spec((dr, d)),
            _const_spec((1, d)),
            _const_spec((1, d)),
        ],
        out_specs=[
            pl.BlockSpec((tc, bg, d), lambda i: (i, 0, 0)),
            _const_spec((bg, dr)),
            _const_spec((CONV_W - 1, bg, dr)),
        ],
        out_shape=[
            jax.ShapeDtypeStruct((t_len, bg, d), F32),
            jax.ShapeDtypeStruct((bg, dr), F32),
            jax.ShapeDtypeStruct((CONV_W - 1, bg, dr), F32),
        ],
        scratch_shapes=[
            pltpu.VMEM((rows + halo, dr), F32),
            pltpu.VMEM((rows, dr), F32),
            pltpu.VMEM((rows, dr), F32),
            pltpu.VMEM((rows, dr), F32),
            pltpu.VMEM((rows, dr), F32),
            pltpu.VMEM((bg, dr), F32),
        ],
        compiler_params=_params("arbitrary"),
        name="rglru_layer",
    )(xt, mods, h0, conv0t, w_in, conv_w, conv_b.reshape(1, dr), wg, b_rg.reshape(1, dr),
      b_ig.reshape(1, dr), lam.reshape(1, dr), w_out, ln_g.reshape(1, d), ln_b.reshape(1, d))


def _proj_kernel(x_ref, mkv_ref, mb_ref, wkv_ref, wqg_ref,
                 k_ref, v_ref, kb_ref, vb_ref, q_ref, g_ref, *, d, da):
    x = x_ref[...]
    u = (x * (1.0 + mkv_ref[:, d:2 * d]) + mkv_ref[:, 0:d]).astype(BF16)
    kv = _dot(u, wkv_ref[...])
    k = kv[:, :da]
    v = kv[:, da:]
    k_ref[...] = k.reshape(k_ref.shape)
    v_ref[...] = v.reshape(v_ref.shape)
    kb_ref[...] = k.astype(BF16)
    vb_ref[...] = v.astype(BF16)
    u2 = (x * (1.0 + mb_ref[:, d:2 * d]) + mb_ref[:, 0:d]).astype(BF16)
    qg = _dot(u2, wqg_ref[...])
    q_ref[...] = (qg[:, :da] * (HEAD_DIM ** -0.5)).astype(BF16)
    gate = qg[:, da:]
    g_ref[...] = gate * _sigmoid(gate)


def _proj(x, mkv, mb, wkv, wqg, *, tm):
    n, d = x.shape
    da = wkv.shape[1] // 2
    groups, mrows, _ = mkv.shape
    per_group = n // groups // tm
    kern = functools.partial(_proj_kernel, d=d, da=da)
    heads = da // HEAD_DIM
    row_spec = pl.BlockSpec((tm, da), lambda g, i: (g * per_group + i, 0))
    head_spec = pl.BlockSpec((tm, heads, HEAD_DIM), lambda g, i: (g * per_group + i, 0, 0))
    return pl.pallas_call(
        kern,
        grid=(groups, per_group),
        in_specs=[
            pl.BlockSpec((tm, d), lambda g, i: (g * per_group + i, 0)),
            pl.BlockSpec((None, mrows, mkv.shape[2]), lambda g, i: (g, 0, 0)),
            pl.BlockSpec((None, mrows, mb.shape[2]), lambda g, i: (g, 0, 0)),
            _const_spec(wkv.shape),
            _const_spec(wqg.shape),
        ],
        out_specs=[head_spec] * 2 + [row_spec] * 4,
        out_shape=[
            jax.ShapeDtypeStruct((n, heads, HEAD_DIM), F32),
            jax.ShapeDtypeStruct((n, heads, HEAD_DIM), F32),
            jax.ShapeDtypeStruct((n, da), BF16),
            jax.ShapeDtypeStruct((n, da), BF16),
            jax.ShapeDtypeStruct((n, da), BF16),
            jax.ShapeDtypeStruct((n, da), F32),
        ],
        compiler_params=_params("parallel", "arbitrary"),
        name="kv_q_proj",
    )(x, mkv, mb, wkv, wqg)


def _attn_kernel(bias_ref, q_ref, k_ref, v_ref, tri_ref, o_ref, c_ref, vm_ref, *, tq, kb, pp):
    grp = pl.program_id(1)
    i = pl.program_id(2)
    lane = lax.broadcasted_iota(jnp.int32, (1, LANES), 1)
    lo_lanes = lane < HEAD_DIM
    nblk = vm_ref.shape[1]
    nh = 2 * pp

    @pl.when(i == 0)
    def _():
        for p in range(pp):
            for j in range(nblk):
                vblk = v_ref[j * kb:(j + 1) * kb, p * LANES:(p + 1) * LANES]
                zero = jnp.zeros_like(vblk)
                vm_ref[p, j, 0:kb, :] = jnp.where(lo_lanes, vblk, zero)
                vm_ref[p, j, kb:2 * kb, :] = jnp.where(lo_lanes, zero, vblk)

    qh = []
    for p in range(pp):
        q = q_ref[:, p * LANES:(p + 1) * LANES]
        zq = jnp.zeros_like(q)
        qh += [jnp.where(lo_lanes, q, zq), jnp.where(lo_lanes, zq, q)]
    bias = [bias_ref[0, nh * grp + h] for h in range(nh)]
    row = lax.broadcasted_iota(jnp.int32, (tq, kb), 0)
    col = lax.broadcasted_iota(jnp.int32, (tq, kb), 1)

    def visit(js, masks):
        ids = [(n, h) for n in range(len(js)) for h in range(nh)]
        kblks = [[k_ref[pl.ds(pl.multiple_of(j * kb, kb), kb), p * LANES:(p + 1) * LANES]
                  for p in range(pp)] for j in js]
        zs = [_dot_nt(qh[h], kblks[n][h // 2]) + bias[h] for n, h in ids]
        sps = [jnp.maximum(z, 0.0) + jnp.log(1.0 + jnp.exp2(jnp.abs(z) * (-LOG2E))) for z in zs]
        if masks is not None:
            sps = [jnp.where(masks[n], sp, 0.0) for (n, h), sp in zip(ids, sps)]
        local = [_dot(sp.astype(BF16), tri_ref[...]) for sp in sps]
        cs = [c_ref[h] for h in range(nh)]
        ws = []
        for (n, h), z, sp, loc in zip(ids, zs, sps, local):
            w = jnp.exp2(((z - sp) - (loc + cs[h])) * LOG2E)
            if masks is not None:
                w = jnp.where(masks[n], w, 0.0)
            ws.append(w.astype(BF16))
            cs[h] = cs[h] + jnp.sum(sp, axis=1, keepdims=True)
        for p in range(pp):
            acc = None
            for n, j in enumerate(js):
                k0 = n * nh + 2 * p
                part = _dot(jnp.concatenate(ws[k0:k0 + 2], axis=1), vm_ref[p, j])
                acc = part if acc is None else acc + part
            o_ref[:, p * LANES:(p + 1) * LANES] += acc
        for h in range(nh):
            c_ref[h] = cs[h]

    o_ref[...] = jnp.zeros_like(o_ref)
    c_ref[...] = jnp.zeros_like(c_ref)
    nd = tq // kb
    visit([nd * i + m for m in reversed(range(nd))], [col + m * kb < row for m in reversed(range(nd))])

    def body(s, carry):
        visit([nd * i - 1 - 2 * s, nd * i - 2 - 2 * s], None)
        return carry

    lax.fori_loop(0, (nd * i) // 2, body, 0)

    if nd % 2 == 1:
        @pl.when(i % 2 == 1)
        def _():
            visit([0], None)


def _attention(q, kbf, vbf, bias, *, batch, tq, kb, pp):
    n, da = q.shape
    t_len = n // batch
    nq = t_len // tq
    groups = da // (pp * LANES)
    tri = (lax.broadcasted_iota(jnp.int32, (kb, kb), 0)
           > lax.broadcasted_iota(jnp.int32, (kb, kb), 1)).astype(BF16)
    kern = functools.partial(_attn_kernel, tq=tq, kb=kb, pp=pp)
    return pl.pallas_call(
        kern,
        grid=(batch, groups, nq),
        in_specs=[
            pl.BlockSpec(memory_space=pltpu.SMEM),
            pl.BlockSpec((tq, pp * LANES), lambda b, g, i: (b * nq + i, g)),
            pl.BlockSpec((t_len, pp * LANES), lambda b, g, i: (b, g)),
            pl.BlockSpec((t_len, pp * LANES), lambda b, g, i: (b, g)),
            _const_spec((kb, kb)),
        ],
        out_specs=pl.BlockSpec((tq, pp * LANES), lambda b, g, i: (b * nq + i, g)),
        out_shape=jax.ShapeDtypeStruct((n, da), F32),
        scratch_shapes=[
            pltpu.VMEM((2 * pp, tq, 1), F32),
            pltpu.VMEM((pp, t_len // kb, 2 * kb, LANES), BF16),
        ],
        compiler_params=_params("parallel", "parallel", "arbitrary"),
        name="sb_attention",
    )(bias, q, kbf, vbf, tri)


def _sattn_kernel(pt_ref, q_ref, kn_ref, vn_ref, bias_ref, tri2_ref, *refs, n_new, page):
    del pt_ref
    k_refs = refs[:PAGES_PER_STEP]
    v_refs = refs[PAGES_PER_STEP:2 * PAGES_PER_STEP]
    o_ref = refs[2 * PAGES_PER_STEP]
    acc_ref, c_ref, qbd_ref, kpad_ref, vpad_ref = refs[2 * PAGES_PER_STEP + 1:]
    g = pl.program_id(1)
    nrow, da = qbd_ref.shape
    heads = da // HEAD_DIM

    def visit(score, weigh, n, valid, carry):
        zs = [score(i) + bias_ref[...] for i in range(n)]
        sps = [_softplus(z) for z in zs]
        if valid is not None:
            sps = [jnp.where(valid, sp, 0.0) for sp in sps]
        local = [_dot(jnp.concatenate(_split_hi_lo(sp), axis=1), tri2_ref[...]) for sp in sps]
        acc = None
        for i in range(n):
            w = jnp.exp((zs[i] - sps[i]) - (local[i] + carry))
            if valid is not None:
                w = jnp.where(valid, w, 0.0)
            part = weigh(i, w.astype(BF16))
            acc = part if acc is None else acc + part
            carry = carry + jnp.sum(sps[i], axis=1, keepdims=True)
        return acc, carry

    @pl.when(g == 0)
    def _():
        q = q_ref[...]
        qrep = jnp.concatenate(
            [jnp.broadcast_to(q[t:t + 1, :], (heads, da)) for t in range(n_new)], axis=0)
        own = (lax.broadcasted_iota(jnp.int32, (nrow, da), 0) % heads
               == lax.broadcasted_iota(jnp.int32, (nrow, da), 1) // HEAD_DIM)
        qbd_ref[...] = jnp.where(own, qrep, 0.0).astype(BF16)
        kpad_ref[...] = jnp.zeros_like(kpad_ref)
        vpad_ref[...] = jnp.zeros_like(vpad_ref)
        kpad_ref[0:SUBLANES, :] = kn_ref[...]
        vpad_ref[0:SUBLANES, :] = vn_ref[...]
        valid = (lax.broadcasted_iota(jnp.int32, (nrow, page), 1)
                 < lax.broadcasted_iota(jnp.int32, (nrow, page), 0) // heads)
        acc_ref[...], c_ref[...] = visit(
            lambda i: _dot_nt(qbd_ref[...], kpad_ref[...].astype(BF16)),
            lambda i, w: _dot(w, vpad_ref[...].astype(BF16)),
            1, valid, jnp.zeros(c_ref.shape, F32))

    def flat(ref):
        return ref[...].reshape(da, page).astype(BF16)

    acc, carry = visit(
        lambda i: _dot(qbd_ref[...], flat(k_refs[i])),
        lambda i, w: _dot_nt(w, flat(v_refs[i])),
        PAGES_PER_STEP, None, c_ref[...])
    acc_ref[...] += acc
    c_ref[...] = carry

    @pl.when(g == pl.num_programs(1) - 1)
    def _():
        own = (lax.broadcasted_iota(jnp.int32, (nrow, da), 0) % heads
               == lax.broadcasted_iota(jnp.int32, (nrow, da), 1) // HEAD_DIM)
        acc = jnp.where(own, acc_ref[...], 0.0)
        o_ref[...] = jnp.zeros_like(o_ref)
        for t in range(n_new):
            o_ref[t:t + 1, :] = jnp.sum(acc[t * heads:(t + 1) * heads, :], axis=0, keepdims=True)


def _sample_attention(q8, kn8, vn8, bias_rows, cache_kt, cache_vt, page_table, *, n_new):
    bs, _, da = q8.shape
    heads, _, page = cache_kt.shape[1:]
    n_pages = page_table.shape[1]
    nrow = n_new * heads
    steps = n_pages // PAGES_PER_STEP
    tri = (lax.broadcasted_iota(jnp.int32, (page, page), 0)
           > lax.broadcasted_iota(jnp.int32, (page, page), 1)).astype(BF16)
    tri2 = jnp.concatenate([tri, tri], axis=0)

    def page_spec(r):
        def index(b, g, pt):
            return (pt[b * n_pages + (n_pages - 1 - (g * PAGES_PER_STEP + r))], 0, 0, 0)
        return pl.BlockSpec((None, heads, HEAD_DIM, page), index)

    tok_spec = pl.BlockSpec((None, SUBLANES, da), lambda b, g, pt: (b, 0, 0))
    kern = functools.partial(_sattn_kernel, n_new=n_new, page=page)
    grid_spec = pltpu.PrefetchScalarGridSpec(
        num_scalar_prefetch=1,
        grid=(bs, steps),
        in_specs=[tok_spec, tok_spec, tok_spec,
                  pl.BlockSpec((nrow, 1), lambda b, g, pt: (0, 0)),
                  pl.BlockSpec((2 * page, page), lambda b, g, pt: (0, 0))]
                 + [page_spec(r) for r in range(PAGES_PER_STEP)] * 2,
        out_specs=tok_spec,
        scratch_shapes=[
            pltpu.VMEM((nrow, da), F32),
            pltpu.VMEM((nrow, 1), F32),
            pltpu.VMEM((nrow, da), BF16),
            pltpu.VMEM((page, da), F32),
            pltpu.VMEM((page, da), F32),
        ],
    )
    return pl.pallas_call(
        kern,
        grid_spec=grid_spec,
        out_shape=jax.ShapeDtypeStruct((bs, SUBLANES, da), F32),
        compiler_params=_params("parallel", "arbitrary"),
        name="sb_attention_paged",
    )(page_table.reshape(-1), q8, kn8, vn8, bias_rows, tri2,
      *([cache_kt] * PAGES_PER_STEP), *([cache_vt] * PAGES_PER_STEP))


def _out_kernel(o_ref, g_ref, x_ref, gc_ref, w_ref, lng_ref, lnb_ref, y_ref, *, alpha):
    y = _dot((o_ref[...] * g_ref[...]).astype(BF16), w_ref[...])
    z = alpha * x_ref[...] + (1.0 + gc_ref[...]) * y
    y_ref[...] = _layer_norm(z, lng_ref[...], lnb_ref[...])


def _out_layer(o, g, x, gate_c, w, ln_g, ln_b, *, alpha, tm):
    n, d = x.shape
    da = o.shape[1]
    groups, mrows, _ = gate_c.shape
    per_group = n // groups // tm
    kern = functools.partial(_out_kernel, alpha=alpha)

    def rows(width):
        return pl.BlockSpec((tm, width), lambda gi, i: (gi * per_group + i, 0))

    return pl.pallas_call(
        kern,
        grid=(groups, per_group),
        in_specs=[rows(da), rows(da), rows(d),
                  pl.BlockSpec((None, mrows, d), lambda gi, i: (gi, 0, 0)),
                  _const_spec(w.shape), _const_spec((1, d)), _const_spec((1, d))],
        out_specs=rows(d),
        out_shape=jax.ShapeDtypeStruct((n, d), F32),
        compiler_params=_params("parallel", "arbitrary"),
        name="attn_out_ln",
    )(o, g, x, gate_c, w, ln_g.reshape(1, d), ln_b.reshape(1, d))


def _largest_divisor(n, cap):
    return max(k for k in range(1, cap + 1) if n % k == 0)


def kernel(x_prompt, x_sample, c_prompt, c_sample, state_lru_h, state_lru_conv, cache_k, cache_v, page_table, ada_w_a, ada_b_a, w_in_a, conv_w, conv_b, w_rg, b_rg, w_ig, b_ig, lru_lambda, w_out_a, ln_g_a, ln_b_a, ada_w_kv, ada_b_kv, w_k, w_v, ada_w_b, ada_b_b, w_in_b, sb_bias, w_out_b, ln_g_b, ln_b_b):
    assert ada_w_a.shape[0] == 1 and ada_w_b.shape[0] == 1, "one RG-LRU and one attention layer"
    bp, t_len, d = x_prompt.shape
    bs, n_new, _ = x_sample.shape
    dr = w_out_a.shape[1]
    da = w_k.shape[1]
    heads = sb_bias.shape[1]
    assert da == heads * HEAD_DIM and w_rg.shape[2] == RG_BLOCK and conv_w.shape[1] == CONV_W
    assert bp % SUBLANES == 0 and bs % SUBLANES == 0 and n_new <= SUBLANES
    alpha = (2.0 *(ada_w_a.shape[0] + ada_w_b.shape[0])) ** 0.25

    c_all = jnp.concatenate([c_prompt, c_sample], axis=0)
    mod_a = _ada(c_all, ada_w_a[0], ada_b_a[0])
    mod_kv = _ada(c_all, ada_w_kv, ada_b_kv)
    mod_b = _ada(c_all, ada_w_b[0], ada_b_b[0])

    w_in = w_in_a[0].astype(BF16)
    wg = jnp.concatenate([w_rg[0], w_ig[0]], axis=-1).astype(BF16)
    w_out = w_out_a[0].astype(BF16)
    wkv = jnp.concatenate([w_k, w_v], axis=1).astype(BF16)
    wqg = w_in_b[0].astype(BF16)
    w_ob = w_out_b[0].astype(BF16)
    rg_args = (w_in, conv_w[0], conv_b[0], wg, b_rg[0], b_ig[0], lru_lambda[0], w_out,
               ln_g_a[0], ln_b_a[0])

    tc = _largest_divisor(t_len, max(1, 512 // bp))
    x1t, h_p, conv_p = _rglru_layer(
        x_prompt.transpose(1, 0, 2), mod_a[:bp], jnp.zeros((bp, dr), F32),
        jnp.zeros((CONV_W - 1, bp, dr), F32), *rg_args, alpha=alpha, tc=tc)
    x1 = x1t.transpose(1, 0, 2).reshape(bp * t_len, d)
    tm = _largest_divisor(t_len, 512)
    k_p, v_p, kb, vb, q, g = _proj(
        x1, mod_kv[:bp, None, :], mod_b[:bp, None, :2 * d], wkv, wqg, tm=tm)
    tq = _largest_divisor(t_len, 512)
    pp = 2 if da % (2 * LANES) == 0 else 1
    o = _attention(q, kb, vb, sb_bias, batch=bp, tq=tq, kb=tq // 2, pp=pp)
    y_p = _out_layer(o, g, x1, mod_b[:bp, None, 2 * d:], w_ob, ln_g_b[0], ln_b_b[0],
                     alpha=alpha, tm=tm)

    ns = n_new * bs
    x1s_t, h_s, conv_s = _rglru_layer(
        x_sample.transpose(1, 0, 2), mod_a[bp:], state_lru_h[0],
        state_lru_conv[0].transpose(1, 0, 2), *rg_args, alpha=alpha, tc=n_new)
    x1s = x1s_t.reshape(ns, d)
    mkv_s = jnp.tile(mod_kv[bp:], (n_new, 1))[None]
    mb_s = jnp.tile(mod_b[bp:], (n_new, 1))[None]
    k_s, v_s, _, _, q_s, g_s = _proj(x1s, mkv_s, mb_s[:, :, :2 * d], wkv, wqg, tm=ns)

    def per_sequence(a):
        a = a.astype(F32).reshape(n_new, bs, da).transpose(1, 0, 2)
        return jnp.pad(a, ((0, 0), (0, SUBLANES - n_new), (0, 0)))

    bias_rows = jnp.tile(sb_bias[0], n_new).reshape(n_new * heads, 1)
    o_s8 = _sample_attention(
        per_sequence(q_s), per_sequence(k_s), per_sequence(v_s), bias_rows,
        cache_k.transpose(0, 2, 3, 1), cache_v.transpose(0, 2, 3, 1), page_table,
        n_new=n_new)
    o_s = o_s8[:, :n_new].transpose(1, 0, 2).reshape(ns, da)
    y_s = _out_layer(o_s, g_s, x1s, mb_s[:, :, 2 * d:], w_ob, ln_g_b[0], ln_b_b[0],
                     alpha=alpha, tm=ns)

    def seq_major(a, *tail):
        return a.reshape(n_new, bs, *tail).transpose(1, 0, *range(2, 2 + len(tail)))

    return (
        y_p.reshape(bp, t_len, d),
        seq_major(y_s, d),
        h_p[None],
        conv_p.transpose(1, 0, 2)[None],
        k_p.reshape(bp, t_len, heads, HEAD_DIM),
        v_p.reshape(bp, t_len, heads, HEAD_DIM),
        h_s[None],
        conv_s.transpose(1, 0, 2)[None],
        seq_major(k_s, heads, HEAD_DIM),
        seq_major(v_s, heads, HEAD_DIM),
    )
```

```python
import functools
import math

import jax
import jax.numpy as jnp
from jax import lax
from jax.experimental import pallas as pl
from jax.experimental.pallas import tpu as pltpu

HEAD_DIM = 64
RG_BLOCK = 128
RG_C = 8.0
CONV_W = 4
LN_EPS = 1e-5
LOG2E = math.log2(math.e)
LANES = 128
SUBLANES = 8
PAGES_PER_STEP = 16
VMEM_LIMIT = 56 * 1024 * 1024

BF16 = jnp.bfloat16
F32 = jnp.float32


def _dot(a, b):
    return jnp.dot(a, b, preferred_element_type=F32)


def _dot_nt(a, b):
    return lax.dot_general(a, b, (((1,), (1,)), ((), ())), preferred_element_type=F32)


def _sigmoid(x):
    return 1.0 / (1.0 + jnp.exp(-x))


def _softplus(x):
    return jnp.maximum(x, 0.0) + jnp.log(1.0 + jnp.exp(-jnp.abs(x)))


def _layer_norm(z, g, b):
    mu = jnp.mean(z, axis=-1, keepdims=True)
    zc = z - mu
    var = jnp.mean(zc * zc, axis=-1, keepdims=True)
    return zc * lax.rsqrt(var + LN_EPS) * g + b


def _split_hi_lo(x):
    hi = x.astype(BF16)
    lo = (x - hi.astype(F32)).astype(BF16)
    return hi, lo


def _params(*sem):
    return pltpu.CompilerParams(dimension_semantics=sem, vmem_limit_bytes=VMEM_LIMIT)


def _const_spec(shape):
    zeros = (0,) * len(shape)
    return pl.BlockSpec(shape, lambda *_: zeros)


def _ada_kernel(c_ref, w_ref, b_ref, o_ref):
    c = c_ref[...]
    s = (c * _sigmoid(c)).astype(BF16)
    o_ref[...] = _dot(s, w_ref[...].astype(BF16)) + b_ref[...]


def _ada(c, w, b):
    n, d = c.shape
    width = w.shape[1]
    tn = 1024 if width % 1024 == 0 else width
    return pl.pallas_call(
        _ada_kernel,
        grid=(width // tn,),
        in_specs=[
            pl.BlockSpec((n, d), lambda j: (0, 0)),
            pl.BlockSpec((d, tn), lambda j: (0, j)),
            pl.BlockSpec((1, tn), lambda j: (0, j)),
        ],
        out_specs=pl.BlockSpec((n, tn), lambda j: (0, j)),
        out_shape=jax.ShapeDtypeStruct((n, width), F32),
        compiler_params=_params("arbitrary"),
        name="ada_mod",
    )(c, w, b.reshape(1, width))


def _rglru_kernel(x_ref, mod_ref, h0_ref, conv0_ref, w_in_ref, convw_ref, convb_ref, wg_ref,
                  brg_ref, big_ref, lam_ref, w_out_ref, lng_ref, lnb_ref,
                  x1_ref, hlast_ref, convlast_ref,
                  xpad_ref, sg_ref, a_ref, b_ref, hs_ref, h_ref, *, tc, bg, d, dr, alpha, seq_major):
    rows = tc * bg
    halo = (CONV_W - 1) * bg

    @pl.when(pl.program_id(0) == 0)
    def _():
        h_ref[...] = h0_ref[...]
        xpad_ref[0:halo, :] = conv0_ref[...].reshape(halo, dr)

    shift = mod_ref[:, 0:d]
    scale = mod_ref[:, d:2 * d]
    gate_c = mod_ref[:, 2 * d:3 * d]
    x = x_ref[...]
    if seq_major:
        x = jnp.swapaxes(x, 0, 1)
    u = x * (1.0 + scale)[None] + shift[None]
    xg = _dot(u.reshape(rows, d).astype(BF16), w_in_ref[...])
    gate = xg[:, dr:]
    sg_ref[...] = gate * _sigmoid(gate)
    xpad_ref[halo:halo + rows, :] = xg[:, :dr]

    lam = lam_ref[...]
    log_sig_lam = -_softplus(-lam)
    for n in range(dr // RG_BLOCK):
        sl = slice(n * RG_BLOCK, (n + 1) * RG_BLOCK)
        xc = convb_ref[:, sl] + sum(
            xpad_ref[k * bg:k * bg + rows, sl] * convw_ref[k:k + 1, sl] for k in range(CONV_W))
        pre = _dot(xc.astype(BF16), wg_ref[n])
        r = _sigmoid(pre[:, :RG_BLOCK] + brg_ref[:, sl])
        ig = _sigmoid(pre[:, RG_BLOCK:] + big_ref[:, sl])
        log_a = (RG_C * r) * log_sig_lam[:, sl]
        a = jnp.exp(log_a)
        a_ref[:, sl] = a
        b_ref[:, sl] = jnp.sqrt(1.0 - a * a) * (ig * xc)

    h = h_ref[...]
    for t in range(tc):
        h = a_ref[t * bg:(t + 1) * bg, :] * h + b_ref[t * bg:(t + 1) * bg, :]
        hs_ref[t * bg:(t + 1) * bg, :] = h
    h_ref[...] = h
    hlast_ref[...] = h
    tail = xpad_ref[rows:rows + halo, :]
    convlast_ref[...] = tail.reshape(CONV_W - 1, bg, dr)
    xpad_ref[0:halo, :] = tail

    y = _dot((hs_ref[...] * sg_ref[...]).astype(BF16), w_out_ref[...])
    z = alpha * x + (1.0 + gate_c)[None] * y.reshape(tc, bg, d)
    x1 = _layer_norm(z, lng_ref[...][None], lnb_ref[...][None])
    x1_ref[...] = jnp.swapaxes(x1, 0, 1) if seq_major else x1


def _rglru_layer(x, mods, h0, conv0t, w_in, conv_w, conv_b, wg, b_rg, b_ig, lam, w_out, ln_g,
                 ln_b, *, alpha, tc, seq_major):
    if seq_major:
        bg, t_len, d = x.shape
        x_spec = pl.BlockSpec((bg, tc, d), lambda i: (0, i, 0))
    else:
        t_len, bg, d = x.shape
        x_spec = pl.BlockSpec((tc, bg, d), lambda i: (i, 0, 0))
    dr = w_out.shape[0]
    rows = tc * bg
    halo = (CONV_W - 1) * bg
    nblk = dr // RG_BLOCK
    kern = functools.partial(_rglru_kernel, tc=tc, bg=bg, d=d, dr=dr, alpha=alpha,
                             seq_major=seq_major)
    return pl.pallas_call(
        kern,
        grid=(t_len // tc,),
        in_specs=[
            x_spec,
            _const_spec((bg, 3 * d)),
            _const_spec((bg, dr)),
            _const_spec((CONV_W - 1, bg, dr)),
            _const_spec((d, 2 * dr)),
            _const_spec((CONV_W, dr)),
            _const_spec((1, dr)),
            _const_spec((nblk, RG_BLOCK, 2 * RG_BLOCK)),
            _const_spec((1, dr)),
            _const_spec((1, dr)),
            _const_spec((1, dr)),
            _const_spec((dr, d)),
            _const_spec((1, d)),
            _const_spec((1, d)),
        ],
        out_specs=[
            x_spec,
            _const_spec((bg, dr)),
            _const_spec((CONV_W - 1, bg, dr)),
        ],
        out_shape=[
            jax.ShapeDtypeStruct(x.shape, F32),
            jax.ShapeDtypeStruct((bg, dr), F32),
            jax.ShapeDtypeStruct((CONV_W - 1, bg, dr), F32),
        ],
        scratch_shapes=[
            pltpu.VMEM((rows + halo, dr), F32),
            pltpu.VMEM((rows, dr), F32),
            pltpu.VMEM((rows, dr), F32),
            pltpu.VMEM((rows, dr), F32),
            pltpu.VMEM((rows, dr), F32),
            pltpu.VMEM((bg, dr), F32),
        ],
        compiler_params=_params("arbitrary"),
        name="rglru_layer",
    )(x, mods, h0, conv0t, w_in, conv_w, conv_b.reshape(1, dr), wg, b_rg.reshape(1, dr),
      b_ig.reshape(1, dr), lam.reshape(1, dr), w_out, ln_g.reshape(1, d), ln_b.reshape(1, d))


def _proj_kernel(x_ref, mkv_ref, mb_ref, wkv_ref, wqg_ref,
                 k_ref, v_ref, kb_ref, vb_ref, q_ref, g_ref, *, d, da):
    x = x_ref[...]
    u = (x * (1.0 + mkv_ref[:, d:2 * d]) + mkv_ref[:, 0:d]).astype(BF16)
    kv = _dot(u, wkv_ref[...])
    k = kv[:, :da]
    v = kv[:, da:]
    k_ref[...] = k.reshape(k_ref.shape)
    v_ref[...] = v.reshape(v_ref.shape)
    kb_ref[...] = k.astype(BF16)
    vb_ref[...] = v.astype(BF16)
    u2 = (x * (1.0 + mb_ref[:, d:2 * d]) + mb_ref[:, 0:d]).astype(BF16)
    qg = _dot(u2, wqg_ref[...])
    q_ref[...] = (qg[:, :da] * (HEAD_DIM ** -0.5)).astype(BF16)
    gate = qg[:, da:]
    g_ref[...] = gate * _sigmoid(gate)


def _proj(x, mkv, mb, wkv, wqg, *, tm):
    n, d = x.shape
    da = wkv.shape[1] // 2
    groups, mrows, _ = mkv.shape
    per_group = n // groups // tm
    kern = functools.partial(_proj_kernel, d=d, da=da)
    heads = da // HEAD_DIM
    row_spec = pl.BlockSpec((tm, da), lambda g, i: (g * per_group + i, 0))
    head_spec = pl.BlockSpec((tm, heads, HEAD_DIM), lambda g, i: (g * per_group + i, 0, 0))
    return pl.pallas_call(
        kern,
        grid=(groups, per_group),
        in_specs=[
            pl.BlockSpec((tm, d), lambda g, i: (g * per_group + i, 0)),
            pl.BlockSpec((None, mrows, mkv.shape[2]), lambda g, i: (g, 0, 0)),
            pl.BlockSpec((None, mrows, mb.shape[2]), lambda g, i: (g, 0, 0)),
            _const_spec(wkv.shape),
            _const_spec(wqg.shape),
        ],
        out_specs=[head_spec] * 2 + [row_spec] * 4,
        out_shape=[
            jax.ShapeDtypeStruct((n, heads, HEAD_DIM), F32),
            jax.ShapeDtypeStruct((n, heads, HEAD_DIM), F32),
            jax.ShapeDtypeStruct((n, da), BF16),
            jax.ShapeDtypeStruct((n, da), BF16),
            jax.ShapeDtypeStruct((n, da), BF16),
            jax.ShapeDtypeStruct((n, da), F32),
        ],
        compiler_params=_params("parallel", "arbitrary"),
        name="kv_q_proj",
    )(x, mkv, mb, wkv, wqg)


def _attn_kernel(bias_ref, q_ref, k_ref, v_ref, tri_ref, o_ref, c_ref, vm_ref, *, tq, kb, pp):
    grp = pl.program_id(1)
    i = pl.program_id(2)
    lane = lax.broadcasted_iota(jnp.int32, (1, LANES), 1)
    lo_lanes = lane < HEAD_DIM
    nblk = vm_ref.shape[1]
    nh = 2 * pp

    @pl.when(i == 0)
    def _():
        for p in range(pp):
            for j in range(nblk):
                vblk = v_ref[j * kb:(j + 1) * kb, p * LANES:(p + 1) * LANES]
                zero = jnp.zeros_like(vblk)
                vm_ref[p, j, 0:kb, :] = jnp.where(lo_lanes, vblk, zero)
                vm_ref[p, j, kb:2 * kb, :] = jnp.where(lo_lanes, zero, vblk)

    qh = []
    for p in range(pp):
        q = q_ref[:, p * LANES:(p + 1) * LANES]
        zq = jnp.zeros_like(q)
        qh += [jnp.where(lo_lanes, q, zq), jnp.where(lo_lanes, zq, q)]
    bias = [bias_ref[0, nh * grp + h] for h in range(nh)]
    row = lax.broadcasted_iota(jnp.int32, (tq, kb), 0)
    col = lax.broadcasted_iota(jnp.int32, (tq, kb), 1)

    def visit(js, masks):
        ids = [(n, h) for n in range(len(js)) for h in range(nh)]
        kblks = [[k_ref[pl.ds(pl.multiple_of(j * kb, kb), kb), p * LANES:(p + 1) * LANES]
                  for p in range(pp)] for j in js]
        zs = [_dot_nt(qh[h], kblks[n][h // 2]) + bias[h] for n, h in ids]
        sps = [jnp.maximum(z, 0.0) + jnp.log(1.0 + jnp.exp2(jnp.abs(z) * (-LOG2E))) for z in zs]
        if masks is not None:
            sps = [jnp.where(masks[n], sp, 0.0) for (n, h), sp in zip(ids, sps)]
        local = [_dot(sp.astype(BF16), tri_ref[...]) for sp in sps]
        cs = [c_ref[h] for h in range(nh)]
        ws = []
        for (n, h), z, sp, loc in zip(ids, zs, sps, local):
            w = jnp.exp2(((z - sp) - (loc + cs[h])) * LOG2E)
            if masks is not None:
                w = jnp.where(masks[n], w, 0.0)
            ws.append(w.astype(BF16))
            cs[h] = cs[h] + jnp.sum(sp, axis=1, keepdims=True)
        for p in range(pp):
            acc = None
            for n, j in enumerate(js):
                k0 = n * nh + 2 * p
                part = _dot(jnp.concatenate(ws[k0:k0 + 2], axis=1), vm_ref[p, j])
                acc = part if acc is None else acc + part
            o_ref[:, p * LANES:(p + 1) * LANES] += acc
        for h in range(nh):
            c_ref[h] = cs[h]

    o_ref[...] = jnp.zeros_like(o_ref)
    c_ref[...] = jnp.zeros_like(c_ref)
    nd = tq // kb
    visit([nd * i + m for m in reversed(range(nd))], [col + m * kb < row for m in reversed(range(nd))])

    def body(s, carry):
        visit([nd * i - 1 - 2 * s, nd * i - 2 - 2 * s], None)
        return carry

    lax.fori_loop(0, (nd * i) // 2, body, 0)

    if nd % 2 == 1:
        @pl.when(i % 2 == 1)
        def _():
            visit([0], None)


def _attention(q, kbf, vbf, bias, *, batch, tq, kb, pp):
    n, da = q.shape
    t_len = n // batch
    nq = t_len // tq
    groups = da // (pp * LANES)
    tri = (lax.broadcasted_iota(jnp.int32, (kb, kb), 0)
           > lax.broadcasted_iota(jnp.int32, (kb, kb), 1)).astype(BF16)
    kern = functools.partial(_attn_kernel, tq=tq, kb=kb, pp=pp)
    return pl.pallas_call(
        kern,
        grid=(batch, groups, nq),
        in_specs=[
            pl.BlockSpec(memory_space=pltpu.SMEM),
            pl.BlockSpec((tq, pp * LANES), lambda b, g, i: (b * nq + i, g)),
            pl.BlockSpec((t_len, pp * LANES), lambda b, g, i: (b, g)),
            pl.BlockSpec((t_len, pp * LANES), lambda b, g, i: (b, g)),
            _const_spec((kb, kb)),
        ],
        out_specs=pl.BlockSpec((tq, pp * LANES), lambda b, g, i: (b * nq + i, g)),
        out_shape=jax.ShapeDtypeStruct((n, da), F32),
        scratch_shapes=[
            pltpu.VMEM((2 * pp, tq, 1), F32),
            pltpu.VMEM((pp, t_len // kb, 2 * kb, LANES), BF16),
        ],
        compiler_params=_params("parallel", "parallel", "arbitrary"),
        name="sb_attention",
    )(bias, q, kbf, vbf, tri)


def _sattn_kernel(pt_ref, q_ref, kn_ref, vn_ref, bias_ref, tri2_ref, *refs, n_new, page):
    del pt_ref
    k_refs = refs[:PAGES_PER_STEP]
    v_refs = refs[PAGES_PER_STEP:2 * PAGES_PER_STEP]
    o_ref = refs[2 * PAGES_PER_STEP]
    acc_ref, c_ref, qbd_ref, kpad_ref, vpad_ref = refs[2 * PAGES_PER_STEP + 1:]
    g = pl.program_id(1)
    nrow, da = qbd_ref.shape
    heads = da // HEAD_DIM

    def visit(score, weigh, n, valid, carry):
        zs = [score(i) + bias_ref[...] for i in range(n)]
        sps = [_softplus(z) for z in zs]
        if valid is not None:
            sps = [jnp.where(valid, sp, 0.0) for sp in sps]
        local = [_dot(jnp.concatenate(_split_hi_lo(sp), axis=1), tri2_ref[...]) for sp in sps]
        acc = None
        for i in range(n):
            w = jnp.exp((zs[i] - sps[i]) - (local[i] + carry))
            if valid is not None:
                w = jnp.where(valid, w, 0.0)
            part = weigh(i, w.astype(BF16))
            acc = part if acc is None else acc + part
            carry = carry + jnp.sum(sps[i], axis=1, keepdims=True)
        return acc, carry

    @pl.when(g == 0)
    def _():
        q = q_ref[...]
        qrep = jnp.concatenate(
            [jnp.broadcast_to(q[t:t + 1, :], (heads, da)) for t in range(n_new)], axis=0)
        own = (lax.broadcasted_iota(jnp.int32, (nrow, da), 0) % heads
               == lax.broadcasted_iota(jnp.int32, (nrow, da), 1) // HEAD_DIM)
        qbd_ref[...] = jnp.where(own, qrep, 0.0).astype(BF16)
        kpad_ref[...] = jnp.zeros_like(kpad_ref)
        vpad_ref[...] = jnp.zeros_like(vpad_ref)
        kpad_ref[0:SUBLANES, :] = kn_ref[...]
        vpad_ref[0:SUBLANES, :] = vn_ref[...]
        valid = (lax.broadcasted_iota(jnp.int32, (nrow, page), 1)
                 < lax.broadcasted_iota(jnp.int32, (nrow, page), 0) // heads)
        acc_ref[...], c_ref[...] = visit(
            lambda i: _dot_nt(qbd_ref[...], kpad_ref[...].astype(BF16)),
            lambda i, w: _dot(w, vpad_ref[...].astype(BF16)),
            1, valid, jnp.zeros(c_ref.shape, F32))

    def flat(ref):
        return ref[...].reshape(da, page).astype(BF16)

    acc, carry = visit(
        lambda i: _dot(qbd_ref[...], flat(k_refs[i])),
        lambda i, w: _dot_nt(w, flat(v_refs[i])),
        PAGES_PER_STEP, None, c_ref[...])
    acc_ref[...] += acc
    c_ref[...] = carry

    @pl.when(g == pl.num_programs(1) - 1)
    def _():
        own = (lax.broadcasted_iota(jnp.int32, (nrow, da), 0) % heads
               == lax.broadcasted_iota(jnp.int32, (nrow, da), 1) // HEAD_DIM)
        acc = jnp.where(own, acc_ref[...], 0.0)
        o_ref[...] = jnp.zeros_like(o_ref)
        for t in range(n_new):
            o_ref[t:t + 1, :] = jnp.sum(acc[t * heads:(t + 1) * heads, :], axis=0, keepdims=True)


def _sample_attention(q8, kn8, vn8, bias_rows, cache_kt, cache_vt, page_table, *, n_new):
    bs, _, da = q8.shape
    heads, _, page = cache_kt.shape[1:]
    n_pages = page_table.shape[1]
    nrow = n_new * heads
    steps = n_pages // PAGES_PER_STEP
    tri = (lax.broadcasted_iota(jnp.int32, (page, page), 0)
           > lax.broadcasted_iota(jnp.int32, (page, page), 1)).astype(BF16)
    tri2 = jnp.concatenate([tri, tri], axis=0)

    def page_spec(r):
        def index(b, g, pt):
            return (pt[b * n_pages + (n_pages - 1 - (g * PAGES_PER_STEP + r))], 0, 0, 0)
        return pl.BlockSpec((None, heads, HEAD_DIM, page), index)

    tok_spec = pl.BlockSpec((None, SUBLANES, da), lambda b, g, pt: (b, 0, 0))
    kern = functools.partial(_sattn_kernel, n_new=n_new, page=page)
    grid_spec = pltpu.PrefetchScalarGridSpec(
        num_scalar_prefetch=1,
        grid=(bs, steps),
        in_specs=[tok_spec, tok_spec, tok_spec,
                  pl.BlockSpec((nrow, 1), lambda b, g, pt: (0, 0)),
                  pl.BlockSpec((2 * page, page), lambda b, g, pt: (0, 0))]
                 + [page_spec(r) for r in range(PAGES_PER_STEP)] * 2,
        out_specs=tok_spec,
        scratch_shapes=[
            pltpu.VMEM((nrow, da), F32),
            pltpu.VMEM((nrow, 1), F32),
            pltpu.VMEM((nrow, da), BF16),
            pltpu.VMEM((page, da), F32),
            pltpu.VMEM((page, da), F32),
        ],
    )
    return pl.pallas_call(
        kern,
        grid_spec=grid_spec,
        out_shape=jax.ShapeDtypeStruct((bs, SUBLANES, da), F32),
        compiler_params=_params("parallel", "arbitrary"),
        name="sb_attention_paged",
    )(page_table.reshape(-1), q8, kn8, vn8, bias_rows, tri2,
      *([cache_kt] * PAGES_PER_STEP), *([cache_vt] * PAGES_PER_STEP))


def _out_kernel(o_ref, g_ref, x_ref, gc_ref, w_ref, lng_ref, lnb_ref, y_ref, *, alpha):
    y = _dot((o_ref[...] * g_ref[...]).astype(BF16), w_ref[...])
    z = alpha * x_ref[...] + (1.0 + gc_ref[...]) * y
    y_ref[...] = _layer_norm(z, lng_ref[...], lnb_ref[...])


def _out_layer(o, g, x, gate_c, w, ln_g, ln_b, *, alpha, tm):
    n, d = x.shape
    da = o.shape[1]
    groups, mrows, _ = gate_c.shape
    per_group = n // groups // tm
    kern = functools.partial(_out_kernel, alpha=alpha)

    def rows(width):
        return pl.BlockSpec((tm, width), lambda gi, i: (gi * per_group + i, 0))

    return pl.pallas_call(
        kern,
        grid=(groups, per_group),
        in_specs=[rows(da), rows(da), rows(d),
                  pl.BlockSpec((None, mrows, d), lambda gi, i: (gi, 0, 0)),
                  _const_spec(w.shape), _const_spec((1, d)), _const_spec((1, d))],
        out_specs=rows(d),
        out_shape=jax.ShapeDtypeStruct((n, d), F32),
        compiler_params=_params("parallel", "arbitrary"),
        name="attn_out_ln",
    )(o, g, x, gate_c, w, ln_g.reshape(1, d), ln_b.reshape(1, d))


def _largest_divisor(n, cap):
    return max(k for k in range(1, cap + 1) if n % k == 0)


def kernel(x_prompt, x_sample, c_prompt, c_sample, state_lru_h, state_lru_conv, cache_k, cache_v, page_table, ada_w_a, ada_b_a, w_in_a, conv_w, conv_b, w_rg, b_rg, w_ig, b_ig, lru_lambda, w_out_a, ln_g_a, ln_b_a, ada_w_kv, ada_b_kv, w_k, w_v, ada_w_b, ada_b_b, w_in_b, sb_bias, w_out_b, ln_g_b, ln_b_b):
    assert ada_w_a.shape[0] == 1 and ada_w_b.shape[0] == 1, "one RG-LRU and one attention layer"
    bp, t_len, d = x_prompt.shape
    bs, n_new, _ = x_sample.shape
    dr = w_out_a.shape[1]
    da = w_k.shape[1]
    heads = sb_bias.shape[1]
    assert da == heads * HEAD_DIM and w_rg.shape[2] == RG_BLOCK and conv_w.shape[1] == CONV_W
    assert bp % SUBLANES == 0 and bs % SUBLANES == 0 and n_new <= SUBLANES
    alpha = (2.0 *(ada_w_a.shape[0] + ada_w_b.shape[0])) ** 0.25

    c_all = jnp.concatenate([c_prompt, c_sample], axis=0)
    mod_a = _ada(c_all, ada_w_a[0], ada_b_a[0])
    mod_kv = _ada(c_all, ada_w_kv, ada_b_kv)
    mod_b = _ada(c_all, ada_w_b[0], ada_b_b[0])

    w_in = w_in_a[0].astype(BF16)
    wg = jnp.concatenate([w_rg[0], w_ig[0]], axis=-1).astype(BF16)
    w_out = w_out_a[0].astype(BF16)
    wkv = jnp.concatenate([w_k, w_v], axis=1).astype(BF16)
    wqg = w_in_b[0].astype(BF16)
    w_ob = w_out_b[0].astype(BF16)
    rg_args = (w_in, conv_w[0], conv_b[0], wg, b_rg[0], b_ig[0], lru_lambda[0], w_out,
               ln_g_a[0], ln_b_a[0])

    tc = _largest_divisor(t_len, max(1, 512 // bp))
    x1, h_p, conv_p = _rglru_layer(
        x_prompt, mod_a[:bp], jnp.zeros((bp, dr), F32),
        jnp.zeros((CONV_W - 1, bp, dr), F32), *rg_args, alpha=alpha, tc=tc, seq_major=True)
    x1 = x1.reshape(bp * t_len, d)
    tm = _largest_divisor(t_len, 512)
    k_p, v_p, kb, vb, q, g = _proj(
        x1, mod_kv[:bp, None, :], mod_b[:bp, None, :2 * d], wkv, wqg, tm=tm)
    tq = _largest_divisor(t_len, 512)
    pp = 2 if da % (2 * LANES) == 0 else 1
    o = _attention(q, kb, vb, sb_bias, batch=bp, tq=tq, kb=tq // 2, pp=pp)
    y_p = _out_layer(o, g, x1, mod_b[:bp, None, 2 * d:], w_ob, ln_g_b[0], ln_b_b[0],
                     alpha=alpha, tm=tm)

    ns = n_new * bs
    x1s_t, h_s, conv_s = _rglru_layer(
        x_sample.transpose(1, 0, 2), mod_a[bp:], state_lru_h[0],
        state_lru_conv[0].transpose(1, 0, 2), *rg_args, alpha=alpha, tc=n_new, seq_major=False)
    x1s = x1s_t.reshape(ns, d)
    mkv_s = jnp.tile(mod_kv[bp:], (n_new, 1))[None]
    mb_s = jnp.tile(mod_b[bp:], (n_new, 1))[None]
    k_s, v_s, _, _, q_s, g_s = _proj(x1s, mkv_s, mb_s[:, :, :2 * d], wkv, wqg, tm=ns)

    def per_sequence(a):
        a = a.astype(F32).reshape(n_new, bs, da).transpose(1, 0, 2)
        return jnp.pad(a, ((0, 0), (0, SUBLANES - n_new), (0, 0)))

    bias_rows = jnp.tile(sb_bias[0], n_new).reshape(n_new * heads, 1)
    o_s8 = _sample_attention(
        per_sequence(q_s), per_sequence(k_s), per_sequence(v_s), bias_rows,
        cache_k.transpose(0, 2, 3, 1), cache_v.transpose(0, 2, 3, 1), page_table,
        n_new=n_new)
    o_s = o_s8[:, :n_new].transpose(1, 0, 2).reshape(ns, da)
    y_s = _out_layer(o_s, g_s, x1s, mb_s[:, :, 2 * d:], w_ob, ln_g_b[0], ln_b_b[0],
                     alpha=alpha, tm=ns)

    def seq_major(a, *tail):
        return a.reshape(n_new, bs, *tail).transpose(1, 0, *range(2, 2 + len(tail)))

    return (
        y_p.reshape(bp, t_len, d),
        seq_major(y_s, d),
        h_p[None],
        conv_p.transpose(1, 0, 2)[None],
        k_p.reshape(bp, t_len, heads, HEAD_DIM),
        v_p.reshape(bp, t_len, heads, HEAD_DIM),
        h_s[None],
        conv_s.transpose(1, 0, 2)[None],
        seq_major(k_s, heads, HEAD_DIM),
        seq_major(v_s, heads, HEAD_DIM),
    )
```
